```python
import math
import jax
import jax.numpy as jnp
from jax import lax
import numpy as np

D_MODEL = 1024
BATCH = 32
SEQ = 2048
DEPTH = 1
DEC_BATCH = 16
DEC_SEQ = 16
PAST_LEN = 2048

CHUNK = 64
A_PAST_CHUNKS = 8
A_BAND = A_PAST_CHUNKS + 1
A_REACH = A_PAST_CHUNKS * CHUNK
N_HEADS_A = 8
HEAD_DIM_A = 64
WIDTH_A = N_HEADS_A * HEAD_DIM_A
REL_CLIP = 128
N_HEADS_B = 4
HEAD_DIM_B = 128
WIDTH_B = N_HEADS_B * HEAD_DIM_B
CONV_W = 4
B_CONV_DIM = 3 * WIDTH_B
N_HEADS_X = 4
HEAD_DIM_X = D_MODEL // N_HEADS_X
N_MEM = 256
D_FF = 2816
EPS = 1e-6
OFF_A = 0
OFF_B = OFF_A + 3 * WIDTH_A
OFF_Z = OFF_B + B_CONV_DIM
OFF_BETA = OFF_Z + WIDTH_B
OFF_DECAY = OFF_BETA + N_HEADS_B
OFF_GATE = OFF_DECAY + N_HEADS_B
N_IN = OFF_GATE + 2 * D_MODEL

kernel_name = 'hybrid_stream_band_attn_gated_delta_step'


def rmsnorm(x, g):
    xf = x.astype(jnp.float32)
    y = xf * lax.rsqrt(jnp.mean(xf * xf, axis=-1, keepdims=True) + EPS)
    return (y * g.astype(jnp.float32)).astype(x.dtype)


def l2norm(x):
    xf = x.astype(jnp.float32)
    return (xf * lax.rsqrt(jnp.sum(xf * xf, axis=-1, keepdims=True) + EPS)).astype(x.dtype)


def swiglu_ffn(x, g, w_gate, w_up, w_down):
    h = rmsnorm(x, g)
    return (jax.nn.silu(h @ w_gate) * (h @ w_up)) @ w_down


def rel_bias(table, d):
    idx = jnp.clip(d, -REL_CLIP, REL_CLIP) + REL_CLIP
    return table[:, idx].astype(jnp.float32)


def band_attention_prompt(q, k, v, table):
    bsz, s, h, dh = q.shape
    nc = s // CHUNK
    pad = ((0, 0), (A_REACH, 0), (0, 0), (0, 0))
    kp = jnp.pad(k, pad)
    vp = jnp.pad(v, pad)
    krel = jnp.arange(A_BAND * CHUNK) - A_REACH
    qrel = jnp.arange(CHUNK)
    bias = rel_bias(table, qrel[:, None] - krel[None, :])
    scale = HEAD_DIM_A ** -0.5

    def one_chunk(c):
        start = c * CHUNK
        qc = lax.dynamic_slice_in_dim(q, start, CHUNK, axis=1)
        kb = lax.dynamic_slice_in_dim(kp, start, A_BAND * CHUNK, axis=1)
        vb = lax.dynamic_slice_in_dim(vp, start, A_BAND * CHUNK, axis=1)
        sc = jnp.einsum('bqhd,bkhd->bhqk', qc, kb).astype(jnp.float32) * scale + bias
        valid = (start + krel) >= 0
        sc = jnp.where(valid[None, None, None, :], sc, -jnp.inf)
        p = jax.nn.softmax(sc, axis=-1).astype(v.dtype)
        return jnp.einsum('bhqk,bkhd->bqhd', p, vb)

    out = lax.map(one_chunk, jnp.arange(nc))
    return out.transpose(1, 0, 2, 3, 4).reshape(bsz, s, h, dh)


def band_attention_sample(q, k_new, v_new, k_cache, v_cache, table):
    t = q.shape[1]
    p_len = k_cache.shape[1]
    kk = jnp.concatenate([k_cache, k_new], axis=1)
    vv = jnp.concatenate([v_cache, v_new], axis=1)
    kpos = jnp.concatenate([jnp.arange(-p_len, 0), jnp.arange(t)])
    bias = rel_bias(table, jnp.arange(t)[:, None] - kpos[None, :])
    sc = jnp.einsum('bqhd,bkhd->bhqk', q, kk).astype(jnp.float32) * (HEAD_DIM_A ** -0.5) + bias
    p = jax.nn.softmax(sc, axis=-1).astype(vv.dtype)
    return jnp.einsum('bhqk,bkhd->bqhd', p, vv)


def causal_conv(x, prev, w):
    t = x.shape[1]
    xp = jnp.concatenate([prev, x], axis=1)
    y = xp[:, 0:t] * w[0]
    for j in range(1, CONV_W):
        y = y + xp[:, j:j + t] * w[j]
    return jax.nn.silu(y), xp[:, -(CONV_W - 1):]


def gated_delta_rule(q, k, v, g, beta, s0, blk):
    bsz, t, h, dk = q.shape
    dv = v.shape[-1]
    n = t // blk
    f32 = jnp.float32

    def blocks(a):
        return a.astype(f32).reshape(bsz, n, blk, h, -1).transpose(1, 0, 3, 2, 4)

    qc, kc, vc = blocks(q), blocks(k), blocks(v)
    gc = jnp.cumsum(blocks(g[..., None])[..., 0], axis=-1)
    bc = blocks(beta[..., None])[..., 0]
    incl = jnp.tril(jnp.ones((blk, blk), bool))
    strict = jnp.tril(jnp.ones((blk, blk), bool), -1)
    decay = jnp.exp(jnp.where(incl, gc[..., :, None] - gc[..., None, :], -jnp.inf))
    kbeta = kc * bc[..., None]
    a_mat = jnp.where(strict, jnp.einsum('nbhid,nbhjd->nbhij', kbeta, kc) * decay, 0.0) + jnp.eye(blk, dtype=f32)
    rhs = jnp.concatenate([vc * bc[..., None], kbeta * jnp.exp(gc)[..., None]], axis=-1)
    sol = lax.linalg.triangular_solve(a_mat, rhs, left_side=True, lower=True, unit_diagonal=True)
    u, w = sol[..., :dv], sol[..., dv:]
    intra = jnp.einsum('nbhid,nbhjd->nbhij', qc, kc) * decay
    qg = qc * jnp.exp(gc)[..., None]
    kdec = kc * jnp.exp(gc[..., -1:] - gc)[..., None]
    glast = jnp.exp(gc[..., -1])

    def step(s, xs):
        u_n, w_n, qg_n, intra_n, kdec_n, gl_n = xs
        v_n = u_n - jnp.einsum('bhlk,bhkv->bhlv', w_n, s)
        o = jnp.einsum('bhlk,bhkv->bhlv', qg_n, s) + jnp.einsum('bhij,bhjv->bhiv', intra_n, v_n)
        s = s * gl_n[..., None, None] + jnp.einsum('bhlk,bhlv->bhkv', kdec_n, v_n)
        return s, o

    s_fin, o = lax.scan(step, s0.astype(f32), (u, w, qg, intra, kdec, glast))
    o = o.transpose(1, 0, 3, 2, 4).reshape(bsz, t, h, dv)
    return o.astype(v.dtype), s_fin.astype(s0.dtype)


def token_mixers(h, lp, a_k_cache, a_v_cache, conv_prev, s0):
    bsz, t, _ = h.shape
    proj = h @ lp['w_in']

    def heads(cols, nh, hd):
        return cols.reshape(bsz, t, nh, hd)

    qa = heads(proj[..., OFF_A:OFF_A + WIDTH_A], N_HEADS_A, HEAD_DIM_A)
    ka = heads(proj[..., OFF_A + WIDTH_A:OFF_A + 2 * WIDTH_A], N_HEADS_A, HEAD_DIM_A)
    va = heads(proj[..., OFF_A + 2 * WIDTH_A:OFF_B], N_HEADS_A, HEAD_DIM_A)
    if a_k_cache is None:
        ya = band_attention_prompt(qa, ka, va, lp['a_rel_bias'])
        keep = min(A_REACH, t)
        new_k, new_v = ka[:, t - keep:], va[:, t - keep:]
    else:
        ya = band_attention_sample(qa, ka, va, a_k_cache, a_v_cache, lp['a_rel_bias'])
        new_k, new_v = ka, va
    xb, new_conv = causal_conv(proj[..., OFF_B:OFF_Z], conv_prev, lp['b_conv_w'])
    qb = l2norm(heads(xb[..., :WIDTH_B], N_HEADS_B, HEAD_DIM_B)) * (HEAD_DIM_B ** -0.5)
    kb = l2norm(heads(xb[..., WIDTH_B:2 * WIDTH_B], N_HEADS_B, HEAD_DIM_B))
    vb = heads(xb[..., 2 * WIDTH_B:], N_HEADS_B, HEAD_DIM_B)
    zb = heads(proj[..., OFF_Z:OFF_BETA], N_HEADS_B, HEAD_DIM_B)
    beta = jax.nn.sigmoid(proj[..., OFF_BETA:OFF_DECAY].astype(jnp.float32))
    decay_in = proj[..., OFF_DECAY:OFF_GATE].astype(jnp.float32) + lp['b_dt_bias'].astype(jnp.float32)
    g = -jnp.exp(lp['b_a_log'].astype(jnp.float32)) * jax.nn.softplus(decay_in)
    ob, new_s = gated_delta_rule(qb, kb, vb, g, beta, s0, min(CHUNK, t))
    ob = rmsnorm(ob, lp['b_out_norm']) * jax.nn.silu(zb)
    gate = jax.nn.sigmoid(proj[..., OFF_GATE:])
    merged = (gate[..., :D_MODEL] * (ya.reshape(bsz, t, WIDTH_A) @ lp['w_branch_a'])
              + gate[..., D_MODEL:] * (ob.reshape(bsz, t, WIDTH_B) @ lp['w_branch_b']))
    return merged @ lp['w_mix_out'], new_k, new_v, new_conv, new_s


def memory_kv(mem, g, wk, wv):
    m = rmsnorm(mem, g)
    bsz = mem.shape[0]
    mk = (m @ wk).reshape(bsz, N_MEM, N_HEADS_X, HEAD_DIM_X)
    mv = (m @ wv).reshape(bsz, N_MEM, N_HEADS_X, HEAD_DIM_X)
    return mk, mv


def cross_attention(x, g, wq, wo, mk, mv):
    bsz, t, _ = x.shape
    q = (rmsnorm(x, g) @ wq).reshape(bsz, t, N_HEADS_X, HEAD_DIM_X)
    sc = jnp.einsum('bqhd,bkhd->bhqk', q, mk).astype(jnp.float32) * (HEAD_DIM_X ** -0.5)
    p = jax.nn.softmax(sc, axis=-1).astype(mv.dtype)
    o = jnp.einsum('bhqk,bkhd->bqhd', p, mv).reshape(bsz, t, D_MODEL)
    return o @ wo


def trunk_layer(x, lp, mem_k, mem_v, a_k_cache, a_v_cache, conv_prev, s0):
    x = x + 0.5 * swiglu_ffn(x, lp['ffn1_norm'], lp['ffn1_w_gate'], lp['ffn1_w_up'], lp['ffn1_w_down'])
    mix, new_k, new_v, new_conv, new_s = token_mixers(rmsnorm(x, lp['mix_norm']), lp, a_k_cache, a_v_cache, conv_prev, s0)
    x = x + mix
    x = x + cross_attention(x, lp['xattn_norm'], lp['xattn_wq'], lp['xattn_wo'], mem_k, mem_v)
    x = x + 0.5 * swiglu_ffn(x, lp['ffn2_norm'], lp['ffn2_w_gate'], lp['ffn2_w_up'], lp['ffn2_w_down'])
    return x, new_k, new_v, new_conv, new_s


def setup_inputs(seed: int = 0) -> dict:
    key = jax.random.key(seed)
    keys = list(jax.random.split(key, 40))

    def nrm(shape, scale=1.0):
        return scale * jax.random.normal(keys.pop(), shape, jnp.float32)

    def gain(shape):
        return 1.0 + nrm(shape, 0.05)

    L = DEPTH
    a_len = min(A_REACH, PAST_LEN)
    dt = jnp.exp(jax.random.uniform(keys.pop(), (L, N_HEADS_B), jnp.float32, math.log(1e-3), math.log(1e-1)))
    a_init = jax.random.uniform(keys.pop(), (L, N_HEADS_B), jnp.float32, 1.0, 16.0)
    return {
        'x_prompt': nrm((BATCH, SEQ, D_MODEL)),
        'x_sample': nrm((DEC_BATCH, DEC_SEQ, D_MODEL)),
        'cache_a_k': nrm((L, DEC_BATCH, a_len, N_HEADS_A, HEAD_DIM_A)),
        'cache_a_v': nrm((L, DEC_BATCH, a_len, N_HEADS_A, HEAD_DIM_A)),
        'state_b_conv': nrm((L, DEC_BATCH, CONV_W - 1, B_CONV_DIM)),
        'state_b_s': nrm((L, DEC_BATCH, N_HEADS_B, HEAD_DIM_B, HEAD_DIM_B), 0.1),
        'cache_mem_k': nrm((L, DEC_BATCH, N_MEM, N_HEADS_X, HEAD_DIM_X)),
        'cache_mem_v': nrm((L, DEC_BATCH, N_MEM, N_HEADS_X, HEAD_DIM_X)),
        'mem_prompt': nrm((BATCH, N_MEM, D_MODEL)),
        'ffn1_norm': gain((L, D_MODEL)),
        'ffn1_w_gate': nrm((L, D_MODEL, D_FF), D_MODEL ** -0.5),
        'ffn1_w_up': nrm((L, D_MODEL, D_FF), D_MODEL ** -0.5),
        'ffn1_w_down': nrm((L, D_FF, D_MODEL), D_FF ** -0.5),
        'mix_norm': gain((L, D_MODEL)),
        'w_in': nrm((L, D_MODEL, N_IN), D_MODEL ** -0.5),
        'a_rel_bias': nrm((L, N_HEADS_A, 2 * REL_CLIP + 1), 0.5),
        'b_conv_w': nrm((L, CONV_W, B_CONV_DIM), CONV_W ** -0.5),
        'b_a_log': jnp.log(a_init),
        'b_dt_bias': dt + jnp.log(-jnp.expm1(-dt)),
        'b_out_norm': gain((L, HEAD_DIM_B)),
        'w_branch_a': nrm((L, WIDTH_A, D_MODEL), WIDTH_A ** -0.5),
        'w_branch_b': nrm((L, WIDTH_B, D_MODEL), WIDTH_B ** -0.5),
        'w_mix_out': nrm((L, D_MODEL, D_MODEL), D_MODEL ** -0.5),
        'xattn_norm': gain((L, D_MODEL)),
        'mem_norm': gain((L, D_MODEL)),
        'xattn_wq': nrm((L, D_MODEL, D_MODEL), D_MODEL ** -0.5),
        'xattn_wk': nrm((L, D_MODEL, D_MODEL), D_MODEL ** -0.5),
        'xattn_wv': nrm((L, D_MODEL, D_MODEL), D_MODEL ** -0.5),
        'xattn_wo': nrm((L, D_MODEL, D_MODEL), D_MODEL ** -0.5),
        'ffn2_norm': gain((L, D_MODEL)),
        'ffn2_w_gate': nrm((L, D_MODEL, D_FF), D_MODEL ** -0.5),
        'ffn2_w_up': nrm((L, D_MODEL, D_FF), D_MODEL ** -0.5),
        'ffn2_w_down': nrm((L, D_FF, D_MODEL), D_FF ** -0.5),
        'final_norm': gain((D_MODEL,)),
    }


def reference(x_prompt, x_sample, cache_a_k, cache_a_v, state_b_conv, state_b_s, cache_mem_k, cache_mem_v,
              mem_prompt, ffn1_norm, ffn1_w_gate, ffn1_w_up, ffn1_w_down, mix_norm, w_in, a_rel_bias,
              b_conv_w, b_a_log, b_dt_bias, b_out_norm, w_branch_a, w_branch_b, w_mix_out, xattn_norm,
              mem_norm, xattn_wq, xattn_wk, xattn_wv, xattn_wo, ffn2_norm, ffn2_w_gate, ffn2_w_up,
              ffn2_w_down, final_norm):
    xp, xs = x_prompt, x_sample
    bp = x_prompt.shape[0]
    p_a_k, p_a_v, p_b_conv, p_b_s, p_mem_k, p_mem_v = [], [], [], [], [], []
    s_a_k, s_a_v, s_b_conv, s_b_s = [], [], [], []
    for l in range(DEPTH):
        lp = {
            'ffn1_norm': ffn1_norm[l], 'ffn1_w_gate': ffn1_w_gate[l], 'ffn1_w_up': ffn1_w_up[l],
            'ffn1_w_down': ffn1_w_down[l], 'mix_norm': mix_norm[l], 'w_in': w_in[l],
            'a_rel_bias': a_rel_bias[l], 'b_conv_w': b_conv_w[l], 'b_a_log': b_a_log[l],
            'b_dt_bias': b_dt_bias[l], 'b_out_norm': b_out_norm[l], 'w_branch_a': w_branch_a[l],
            'w_branch_b': w_branch_b[l], 'w_mix_out': w_mix_out[l], 'xattn_norm': xattn_norm[l],
            'xattn_wq': xattn_wq[l], 'xattn_wo': xattn_wo[l], 'ffn2_norm': ffn2_norm[l],
            'ffn2_w_gate': ffn2_w_gate[l], 'ffn2_w_up': ffn2_w_up[l], 'ffn2_w_down': ffn2_w_down[l],
        }
        mk_p, mv_p = memory_kv(mem_prompt, mem_norm[l], xattn_wk[l], xattn_wv[l])
        conv0 = jnp.zeros((bp, CONV_W - 1, B_CONV_DIM), x_prompt.dtype)
        s_zero = jnp.zeros((bp, N_HEADS_B, HEAD_DIM_B, HEAD_DIM_B), x_prompt.dtype)
        xp, ak, av, cv, sb = trunk_layer(xp, lp, mk_p, mv_p, None, None, conv0, s_zero)
        p_a_k.append(ak)
        p_a_v.append(av)
        p_b_conv.append(cv)
        p_b_s.append(sb)
        p_mem_k.append(mk_p)
        p_mem_v.append(mv_p)
        xs, ak, av, cv, sb = trunk_layer(xs, lp, cache_mem_k[l], cache_mem_v[l], cache_a_k[l], cache_a_v[l],
                                         state_b_conv[l], state_b_s[l])
        s_a_k.append(ak)
        s_a_v.append(av)
        s_b_conv.append(cv)
        s_b_s.append(sb)
    y_prompt = rmsnorm(xp, final_norm)
    y_sample = rmsnorm(xs, final_norm)
    return (y_prompt, y_sample,
            jnp.stack(p_a_k), jnp.stack(p_a_v), jnp.stack(p_b_conv), jnp.stack(p_b_s),
            jnp.stack(p_mem_k), jnp.stack(p_mem_v),
            jnp.stack(s_a_k), jnp.stack(s_a_v), jnp.stack(s_b_conv), jnp.stack(s_b_s))
```

```python
import functools

import jax
import jax.numpy as jnp
from jax import lax
from jax.experimental import pallas as pl
from jax.experimental.pallas import tpu as pltpu

F32 = jnp.float32
BF16 = jnp.bfloat16
EPS = 1e-6

CHUNK = 64
A_REACH = 8 * CHUNK
REL_CLIP = 128
N_HEADS_A = 8
HEAD_DIM_A = 64
WIDTH_A = N_HEADS_A * HEAD_DIM_A
N_HEADS_B = 4
HEAD_DIM_B = 128
WIDTH_B = N_HEADS_B * HEAD_DIM_B
CONV_W = 4
B_CONV_DIM = 3 * WIDTH_B
N_HEADS_X = 4
LANES = 128

VMEM_LIMIT_BYTES = 56 * 1024 * 1024


def _params(n_grid_axes):
    return pltpu.CompilerParams(
        dimension_semantics=("arbitrary",) * n_grid_axes,
        vmem_limit_bytes=VMEM_LIMIT_BYTES,
    )


def _resident(shape):
    nd = len(shape)
    return pl.BlockSpec(shape, lambda *_: (0,) * nd, pipeline_mode=pl.Buffered(1))


def _rms(x, g):
    return x * lax.rsqrt(jnp.mean(x * x, axis=-1, keepdims=True) + EPS) * g


def _silu(x):
    return x * jax.nn.sigmoid(x)


def _softplus(x):
    return jnp.maximum(x, 0.0) + jnp.log1p(jnp.exp(-jnp.abs(x)))


def _dot(a, b):
    return jnp.dot(a, b, preferred_element_type=F32)


def _dot_nt(a, b):
    return lax.dot_general(a, b, (((1,), (1,)), ((), ())), preferred_element_type=F32)


def _split2(x):
    hi = x.astype(BF16)
    lo = (x - hi.astype(F32)).astype(BF16)
    return hi, lo


def _dot_x3(a, b):
    ah, al = _split2(a)
    bh, bl = _split2(b)
    return _dot(ah, bh) + _dot(ah, bl) + _dot(al, bh)


def _token_tile(n, want):
    t = min(want, n)
    while n % t:
        t //= 2
    return t


def _ffn_body(*refs, final):
    if final:
        x_ref, g_ref, wg_ref, wu_ref, wd_ref, fg_ref, o_ref = refs
    else:
        x_ref, g_ref, wg_ref, wu_ref, wd_ref, o_ref = refs
    x = x_ref[...]
    h = _rms(x, g_ref[...]).astype(BF16)
    gate = _dot(h, wg_ref[...])
    up = _dot(h, wu_ref[...])
    a = (_silu(gate) * up).astype(BF16)
    y = x + 0.5 * _dot(a, wd_ref[...])
    if final:
        y = _rms(y, fg_ref[...])
    o_ref[...] = y


def _ffn(x, g, wg, wu, wd, final_g=None, tile=512):
    n, d = x.shape
    f = wg.shape[1]
    tm = _token_tile(n, tile)
    final = final_g is not None
    in_specs = [
        pl.BlockSpec((tm, d), lambda i: (i, 0)),
        _resident((1, d)),
        _resident((d, f)),
        _resident((d, f)),
        _resident((f, d)),
    ]
    args = [x, g, wg, wu, wd]
    if final:
        in_specs.append(_resident((1, d)))
        args.append(final_g)
    return pl.pallas_call(
        functools.partial(_ffn_body, final=final),
        out_shape=jax.ShapeDtypeStruct((n, d), F32),
        grid=(n // tm,),
        in_specs=in_specs,
        out_specs=pl.BlockSpec((tm, d), lambda i: (i, 0)),
        compiler_params=_params(1),
        name="ffn_final" if final else "ffn",
    )(*args)


_PROJ_WIDTHS = (WIDTH_A, WIDTH_A, WIDTH_A, B_CONV_DIM, WIDTH_B, None, LANES)


def _proj_body(x_ref, g_ref, w_ref, *out_refs, widths):
    h = _rms(x_ref[...], g_ref[...]).astype(BF16)
    off = 0
    for o_ref, wdt in zip(out_refs, widths):
        o_ref[...] = _dot(h, w_ref[:, off:off + wdt])
        off += wdt


def _proj(x, g, w_cat, tile=512):
    n, d = x.shape
    widths = tuple(2 * d if w is None else w for w in _PROJ_WIDTHS)
    assert sum(widths) == w_cat.shape[1]
    tm = _token_tile(n, tile)
    return pl.pallas_call(
        functools.partial(_proj_body, widths=widths),
        out_shape=tuple(jax.ShapeDtypeStruct((n, w), F32) for w in widths),
        grid=(n // tm,),
        in_specs=[
            pl.BlockSpec((tm, d), lambda i: (i, 0)),
            _resident((1, d)),
            _resident(w_cat.shape),
        ],
        out_specs=tuple(pl.BlockSpec((tm, w), lambda i: (i, 0)) for w in widths),
        compiler_params=_params(1),
        name="in_proj",
    )(x, g, w_cat)


def _pair_masks(rows):
    lane = lax.broadcasted_iota(jnp.int32, (rows, LANES), 1)
    return lane < HEAD_DIM_A


def _attn_prompt_body(q_ref, k_ref, v_ref, bias_ref, o_ref, kpad, vpad):
    s_len = q_ref.shape[1]
    nc = s_len // CHUNK
    band = A_REACH + CHUNK
    scale = HEAD_DIM_A ** -0.5

    kpad[0:A_REACH, :] = jnp.zeros((A_REACH, WIDTH_A), BF16)
    vpad[0:A_REACH, :] = jnp.zeros((A_REACH, WIDTH_A), BF16)

    def fill(i, carry):
        r = pl.multiple_of(i * 256, 256)
        kpad[pl.ds(A_REACH + r, 256), :] = k_ref[0, pl.ds(r, 256), :].astype(BF16)
        vpad[pl.ds(A_REACH + r, 256), :] = v_ref[0, pl.ds(r, 256), :].astype(BF16)
        return carry

    lax.fori_loop(0, s_len // 256, fill, 0)

    low = _pair_masks(CHUNK)
    kcol = lax.broadcasted_iota(jnp.int32, (CHUNK, band), 1)

    def chunk(c, carry):
        start = pl.multiple_of(c * CHUNK, CHUNK)
        valid = (kcol + start) >= A_REACH
        for hp in range(N_HEADS_A // 2):
            sl = slice(hp * LANES, (hp + 1) * LANES)
            qp = q_ref[0, pl.ds(start, CHUNK), sl] * scale
            kb = kpad[pl.ds(start, band), sl]
            vb = vpad[pl.ds(start, band), sl]
            outs = []
            for half in range(2):
                qm = jnp.where(low if half == 0 else ~low, qp, 0.0).astype(BF16)
                s = _dot_nt(qm, kb) + bias_ref[2 * hp + half]
                s = jnp.where(valid, s, -jnp.inf)
                m = jnp.max(s, axis=-1, keepdims=True)
                p = jnp.exp(s - m)
                p = p * (1.0 / jnp.sum(p, axis=-1, keepdims=True))
                outs.append(_dot(p.astype(BF16), vb))
            o_ref[0, pl.ds(start, CHUNK), sl] = jnp.where(low, outs[0], outs[1])
        return carry

    lax.fori_loop(0, nc, chunk, 0)


def _attn_prompt(q, k, v, bias):
    b, s_len, w = q.shape
    band = A_REACH + CHUNK
    seq = pl.BlockSpec((1, s_len, w), lambda i: (i, 0, 0))
    return pl.pallas_call(
        _attn_prompt_body,
        out_shape=jax.ShapeDtypeStruct((b, s_len, w), F32),
        grid=(b,),
        in_specs=[seq, seq, seq, _resident((N_HEADS_A, CHUNK, band))],
        out_specs=seq,
        scratch_shapes=[
            pltpu.VMEM((A_REACH + s_len, w), BF16),
            pltpu.VMEM((A_REACH + s_len, w), BF16),
        ],
        compiler_params=_params(1),
        name="band_attn_prompt",
    )(q, k, v, bias)


def _attn_sample_body(q_ref, k_ref, v_ref, ck_ref, cv_ref, bc_ref, bn_ref, o_ref):
    t = q_ref.shape[1]
    scale = HEAD_DIM_A ** -0.5
    low = _pair_masks(t)
    for hp in range(N_HEADS_A // 2):
        sl = slice(hp * LANES, (hp + 1) * LANES)
        qp = q_ref[0, :, sl] * scale
        kn = k_ref[0, :, sl].astype(BF16)
        vn = v_ref[0, :, sl].astype(BF16)
        kc = ck_ref[0, :, sl].astype(BF16)
        vc = cv_ref[0, :, sl].astype(BF16)
        outs = []
        for half in range(2):
            h = 2 * hp + half
            qm = jnp.where(low if half == 0 else ~low, qp, 0.0).astype(BF16)
            sc = _dot_nt(qm, kc) + bc_ref[h]
            sn = _dot_nt(qm, kn) + bn_ref[h]
            m = jnp.maximum(jnp.max(sc, axis=-1, keepdims=True), jnp.max(sn, axis=-1, keepdims=True))
            pc = jnp.exp(sc - m)
            pn = jnp.exp(sn - m)
            inv = 1.0 / (jnp.sum(pc, axis=-1, keepdims=True) + jnp.sum(pn, axis=-1, keepdims=True))
            outs.append(_dot((pc * inv).astype(BF16), vc) + _dot((pn * inv).astype(BF16), vn))
        o_ref[0, :, sl] = jnp.where(low, outs[0], outs[1])


def _attn_sample(q, k, v, ck, cv, bias_cache, bias_new):
    b, t, w = q.shape
    p_len = ck.shape[1]
    new = pl.BlockSpec((1, t, w), lambda i: (i, 0, 0))
    old = pl.BlockSpec((1, p_len, w), lambda i: (i, 0, 0))
    return pl.pallas_call(
        _attn_sample_body,
        out_shape=jax.ShapeDtypeStruct((b, t, w), F32),
        grid=(b,),
        in_specs=[new, new, new, old, old, _resident(bias_cache.shape), _resident(bias_new.shape)],
        out_specs=new,
        compiler_params=_params(1),
        name="band_attn_sample",
    )(q, k, v, ck, cv, bias_cache, bias_new)


def _unit_lower_inverse(a_strict, blk):
    row = lax.broadcasted_iota(jnp.int32, (blk, blk), 0)
    col = lax.broadcasted_iota(jnp.int32, (blk, blk), 1)
    eye = jnp.where(row == col, 1.0, 0.0).astype(F32)
    p = -a_strict
    t = eye + p
    span = 2
    while span < blk:
        p = _dot_x3(p, p)
        t = t + _dot_x3(t, p)
        span *= 2
    return t


def _delta_body(xb_ref, bd_ref, prev_ref, s0_ref, cw_ref, alog_ref, dtb_ref,
                o_ref, conv_ref, s_ref, xp, *, blk):
    t_len = xb_ref.shape[1]
    n_blk = t_len // blk
    hd = HEAD_DIM_B
    pad = 8

    xp[0:pad, :] = jnp.zeros((pad, B_CONV_DIM), F32)
    xp[pad - (CONV_W - 1):pad, :] = prev_ref[0]
    xp[pad:pad + t_len, :] = xb_ref[0]
    conv_ref[0] = xp[pad + t_len - (CONV_W - 1):pad + t_len, :]
    s_ref[0] = s0_ref[0]

    row = lax.broadcasted_iota(jnp.int32, (blk, blk), 0)
    col = lax.broadcasted_iota(jnp.int32, (blk, blk), 1)
    incl = col <= row
    strict = col < row
    tril = jnp.where(incl, 1.0, 0.0).astype(BF16)
    neg_a = -jnp.exp(alog_ref[...])

    def block(n, carry):
        t0 = pl.multiple_of(n * blk, blk)
        win = xp[pl.ds(t0, blk + pad), :]
        y = win[pad:pad + blk] * cw_ref[CONV_W - 1:CONV_W, :]
        for j in range(CONV_W - 1):
            lo = pad - (CONV_W - 1) + j
            y = y + win[lo:lo + blk] * cw_ref[j:j + 1, :]
        xc = _silu(y)
        bd = bd_ref[0, pl.ds(t0, blk), :]
        beta_all = jax.nn.sigmoid(bd)
        g_all = neg_a * _softplus(bd + dtb_ref[...])
        for h in range(N_HEADS_B):
            q = xc[:, h * hd:(h + 1) * hd]
            k = xc[:, WIDTH_B + h * hd:WIDTH_B + (h + 1) * hd]
            v = xc[:, 2 * WIDTH_B + h * hd:2 * WIDTH_B + (h + 1) * hd]
            q = q * lax.rsqrt(jnp.sum(q * q, axis=-1, keepdims=True) + EPS) * (hd ** -0.5)
            k = k * lax.rsqrt(jnp.sum(k * k, axis=-1, keepdims=True) + EPS)
            beta = jnp.broadcast_to(beta_all[:, h:h + 1], (blk, hd))
            g = jnp.broadcast_to(g_all[:, N_HEADS_B + h:N_HEADS_B + h + 1], (blk, hd))
            g1 = g.astype(BF16)
            r1 = g - g1.astype(F32)
            g2 = r1.astype(BF16)
            g3 = (r1 - g2.astype(F32)).astype(BF16)
            gc = _dot(tril, g1) + _dot(tril, g2) + _dot(tril, g3)
            gc_i = gc[:, 0:blk]
            gc_j = gc.T[0:blk, :]
            decay = jnp.exp(jnp.where(incl, gc_i - gc_j, -jnp.inf))
            eg = jnp.exp(gc)
            g_last = gc[blk - 1:blk, :]
            kbeta = k * beta
            kb16 = k.astype(BF16)
            a_strict = jnp.where(strict, _dot_nt(kbeta.astype(BF16), kb16) * decay, 0.0)
            t_inv = _unit_lower_inverse(a_strict, blk)
            u = _dot_x3(t_inv, v * beta)
            w = _dot_x3(t_inv, kbeta * eg)
            intra = _dot_nt(q.astype(BF16), kb16) * decay
            qg = q * eg
            kdec = k * jnp.exp(g_last - gc)
            s = s_ref[0, h]
            s16 = s.astype(BF16)
            v_n = u - _dot(w.astype(BF16), s16)
            v16 = v_n.astype(BF16)
            o = _dot(qg.astype(BF16), s16) + _dot(intra.astype(BF16), v16)
            s_ref[0, h] = s * jnp.exp(g_last) + _dot(kdec.T.astype(BF16), v16)
            o_ref[0, pl.ds(t0, blk), h * hd:(h + 1) * hd] = o
        return carry

    lax.fori_loop(0, n_blk, block, 0)


def _delta(xb, bd, prev, s0, conv_w, alog_row, dtb_row, blk):
    b, t, cdim = xb.shape
    return pl.pallas_call(
        functools.partial(_delta_body, blk=blk),
        out_shape=(
            jax.ShapeDtypeStruct((b, t, WIDTH_B), F32),
            jax.ShapeDtypeStruct((b, CONV_W - 1, cdim), F32),
            jax.ShapeDtypeStruct((b, N_HEADS_B, HEAD_DIM_B, HEAD_DIM_B), F32),
        ),
        grid=(b,),
        in_specs=[
            pl.BlockSpec((1, t, cdim), lambda i: (i, 0, 0)),
            pl.BlockSpec((1, t, LANES), lambda i: (i, 0, 0)),
            pl.BlockSpec((1, CONV_W - 1, cdim), lambda i: (i, 0, 0)),
            pl.BlockSpec((1, N_HEADS_B, HEAD_DIM_B, HEAD_DIM_B), lambda i: (i, 0, 0, 0)),
            _resident((CONV_W, cdim)),
            _resident((1, LANES)),
            _resident((1, LANES)),
        ],
        out_specs=(
            pl.BlockSpec((1, t, WIDTH_B), lambda i: (i, 0, 0)),
            pl.BlockSpec((1, CONV_W - 1, cdim), lambda i: (i, 0, 0)),
            pl.BlockSpec((1, N_HEADS_B, HEAD_DIM_B, HEAD_DIM_B), lambda i: (i, 0, 0, 0)),
        ),
        scratch_shapes=[pltpu.VMEM((t + 8, cdim), F32)],
        compiler_params=_params(1),
        name="gated_delta",
    )(xb, bd, prev, s0, conv_w, alog_row, dtb_row)


def _merge_body(x_ref, ya_ref, ob_ref, z_ref, gate_ref, on_ref, wa_ref, wb_ref, wm_ref, o_ref):
    d = x_ref.shape[1]
    hd = HEAD_DIM_B
    ob = ob_ref[...]
    z = z_ref[...]
    parts = []
    for h in range(N_HEADS_B):
        oh = ob[:, h * hd:(h + 1) * hd]
        parts.append(_rms(oh, on_ref[...]) * _silu(z[:, h * hd:(h + 1) * hd]))
    obn = jnp.concatenate(parts, axis=-1).astype(BF16)
    gate = jax.nn.sigmoid(gate_ref[...])
    merged = (gate[:, :d] * _dot(ya_ref[...].astype(BF16), wa_ref[...])
              + gate[:, d:] * _dot(obn, wb_ref[...]))
    o_ref[...] = x_ref[...] + _dot(merged.astype(BF16), wm_ref[...])


def _merge(x, ya, ob, z, gates, out_norm, wa, wb, wm, tile=512):
    n, d = x.shape
    tm = _token_tile(n, tile)

    def tok(w):
        return pl.BlockSpec((tm, w), lambda i: (i, 0))

    return pl.pallas_call(
        _merge_body,
        out_shape=jax.ShapeDtypeStruct((n, d), F32),
        grid=(n // tm,),
        in_specs=[tok(d), tok(WIDTH_A), tok(WIDTH_B), tok(WIDTH_B), tok(2 * d),
                  _resident((1, HEAD_DIM_B)), _resident(wa.shape), _resident(wb.shape), _resident(wm.shape)],
        out_specs=tok(d),
        compiler_params=_params(1),
        name="merge",
    )(x, ya, ob, z, gates, out_norm, wa, wb, wm)


def _memkv_body(m_ref, g_ref, wk_ref, wv_ref, k_ref, v_ref):
    h = _rms(m_ref[...], g_ref[...]).astype(BF16)
    k_ref[...] = _dot(h, wk_ref[...])
    v_ref[...] = _dot(h, wv_ref[...])


def _memkv(mem, g, wk, wv, tile=512):
    n, d = mem.shape
    tm = _token_tile(n, tile)
    tok = pl.BlockSpec((tm, d), lambda i: (i, 0))
    return pl.pallas_call(
        _memkv_body,
        out_shape=(jax.ShapeDtypeStruct((n, d), F32),) * 2,
        grid=(n // tm,),
        in_specs=[tok, _resident((1, d)), _resident(wk.shape), _resident(wv.shape)],
        out_specs=(tok, tok),
        compiler_params=_params(1),
        name="mem_kv",
    )(mem, g, wk, wv)


def _xattn_body(x_ref, mk_ref, mv_ref, g_ref, wq_ref, wo_ref, o_ref):
    x = x_ref[0]
    d = x.shape[1]
    hd = d // N_HEADS_X
    q = _dot(_rms(x, g_ref[...]).astype(BF16), wq_ref[...]) * (hd ** -0.5)
    outs = []
    for h in range(N_HEADS_X):
        sl = slice(h * hd, (h + 1) * hd)
        s = _dot_nt(q[:, sl].astype(BF16), mk_ref[0, :, sl].astype(BF16))
        m = jnp.max(s, axis=-1, keepdims=True)
        p = jnp.exp(s - m)
        p = p * (1.0 / jnp.sum(p, axis=-1, keepdims=True))
        outs.append(_dot(p.astype(BF16), mv_ref[0, :, sl].astype(BF16)))
    o = jnp.concatenate(outs, axis=-1).astype(BF16)
    o_ref[0] = x + _dot(o, wo_ref[...])


def _xattn(x, mk, mv, g, wq, wo, tile=512):
    b, t, d = x.shape
    n_mem = mk.shape[1]
    tq = _token_tile(t, tile)
    tok = pl.BlockSpec((1, tq, d), lambda i, j: (i, j, 0))
    mem = pl.BlockSpec((1, n_mem, d), lambda i, j: (i, 0, 0))
    return pl.pallas_call(
        _xattn_body,
        out_shape=jax.ShapeDtypeStruct((b, t, d), F32),
        grid=(b, t // tq),
        in_specs=[tok, mem, mem, _resident((1, d)), _resident(wq.shape), _resident(wo.shape)],
        out_specs=tok,
        compiler_params=_params(2),
        name="cross_attn",
    )(x, mk, mv, g, wq, wo)


def _rel_bias(table, q_len, k_pos):
    d = jnp.arange(q_len)[:, None] - k_pos[None, :]
    return table[:, jnp.clip(d, -REL_CLIP, REL_CLIP) + REL_CLIP].astype(F32)


def _trunk(x, lp, mem_k, mem_v, a_k_cache, a_v_cache, conv_prev, s0):
    b, t, d = x.shape
    n = b * t
    x = _ffn(x.reshape(n, d), lp["ffn1_norm"], lp["ffn1_wg"], lp["ffn1_wu"], lp["ffn1_wd"])
    qa, ka, va, xb, z, gates, bd = _proj(x, lp["mix_norm"], lp["w_cat"])
    qa, ka, va = (a.reshape(b, t, WIDTH_A) for a in (qa, ka, va))
    if a_k_cache is None:
        k_pos = jnp.arange(A_REACH + CHUNK) - A_REACH
        ya = _attn_prompt(qa, ka, va, _rel_bias(lp["a_rel_bias"], CHUNK, k_pos))
        keep = min(A_REACH, t)
        new_k, new_v = ka[:, t - keep:], va[:, t - keep:]
    else:
        p_len = a_k_cache.shape[1]
        ya = _attn_sample(
            qa, ka, va, a_k_cache.reshape(b, p_len, WIDTH_A), a_v_cache.reshape(b, p_len, WIDTH_A),
            _rel_bias(lp["a_rel_bias"], t, jnp.arange(-p_len, 0)),
            _rel_bias(lp["a_rel_bias"], t, jnp.arange(t)))
        new_k, new_v = ka, va
    ob, new_conv, new_s = _delta(
        xb.reshape(b, t, B_CONV_DIM), bd.reshape(b, t, LANES), conv_prev, s0,
        lp["b_conv_w"], lp["alog_row"], lp["dtb_row"], min(CHUNK, t))
    x = _merge(x, ya.reshape(n, WIDTH_A), ob.reshape(n, WIDTH_B), z, gates, lp["b_out_norm"],
               lp["w_branch_a"], lp["w_branch_b"], lp["w_mix_out"])
    x = _xattn(x.reshape(b, t, d), mem_k, mem_v, lp["xattn_norm"], lp["xattn_wq"], lp["xattn_wo"])
    y = _ffn(x.reshape(n, d), lp["ffn2_norm"], lp["ffn2_wg"], lp["ffn2_wu"], lp["ffn2_wd"],
             final_g=lp["final_norm"])
    heads = (b, -1, N_HEADS_A, HEAD_DIM_A)
    return y.reshape(b, t, d), new_k.reshape(heads), new_v.reshape(heads), new_conv, new_s


def kernel(x_prompt, x_sample, cache_a_k, cache_a_v, state_b_conv, state_b_s, cache_mem_k, cache_mem_v, mem_prompt, ffn1_norm, ffn1_w_gate, ffn1_w_up, ffn1_w_down, mix_norm, w_in, a_rel_bias, b_conv_w, b_a_log, b_dt_bias, b_out_norm, w_branch_a, w_branch_b, w_mix_out, xattn_norm, mem_norm, xattn_wq, xattn_wk, xattn_wv, xattn_wo, ffn2_norm, ffn2_w_gate, ffn2_w_up, ffn2_w_down, final_norm):
    depth = ffn1_norm.shape[0]
    assert depth == 1
    l = 0
    d = x_prompt.shape[-1]
    bp = x_prompt.shape[0]
    n_mem = mem_prompt.shape[1]

    w = w_in[l]
    off_b = 3 * WIDTH_A
    off_z = off_b + B_CONV_DIM
    off_beta = off_z + WIDTH_B
    off_gate = off_beta + 2 * N_HEADS_B
    small = jnp.pad(w[:, off_beta:off_gate], ((0, 0), (0, LANES - 2 * N_HEADS_B)))
    w_cat = jnp.concatenate([w[:, :off_beta], w[:, off_gate:], small], axis=1).astype(BF16)

    def lane_row(vec):
        return jnp.pad(vec.astype(F32), (N_HEADS_B, LANES - 2 * N_HEADS_B)).reshape(1, LANES)

    def row(vec):
        return vec.astype(F32).reshape(1, -1)

    lp = {
        "ffn1_norm": row(ffn1_norm[l]), "ffn1_wg": ffn1_w_gate[l].astype(BF16),
        "ffn1_wu": ffn1_w_up[l].astype(BF16), "ffn1_wd": ffn1_w_down[l].astype(BF16),
        "mix_norm": row(mix_norm[l]), "w_cat": w_cat, "a_rel_bias": a_rel_bias[l],
        "b_conv_w": b_conv_w[l], "alog_row": lane_row(b_a_log[l]), "dtb_row": lane_row(b_dt_bias[l]),
        "b_out_norm": row(b_out_norm[l]), "w_branch_a": w_branch_a[l].astype(BF16),
        "w_branch_b": w_branch_b[l].astype(BF16), "w_mix_out": w_mix_out[l].astype(BF16),
        "xattn_norm": row(xattn_norm[l]), "xattn_wq": xattn_wq[l].astype(BF16),
        "xattn_wo": xattn_wo[l].astype(BF16), "ffn2_norm": row(ffn2_norm[l]),
        "ffn2_wg": ffn2_w_gate[l].astype(BF16), "ffn2_wu": ffn2_w_up[l].astype(BF16),
        "ffn2_wd": ffn2_w_down[l].astype(BF16), "final_norm": row(final_norm),
    }

    mk_p, mv_p = _memkv(mem_prompt.reshape(bp * n_mem, d), row(mem_norm[l]),
                        xattn_wk[l].astype(BF16), xattn_wv[l].astype(BF16))
    mk_p = mk_p.reshape(bp, n_mem, d)
    mv_p = mv_p.reshape(bp, n_mem, d)
    conv0 = jnp.zeros((bp, CONV_W - 1, B_CONV_DIM), F32)
    s_zero = jnp.zeros((bp, N_HEADS_B, HEAD_DIM_B, HEAD_DIM_B), F32)
    yp, p_ak, p_av, p_cv, p_sb = _trunk(x_prompt, lp, mk_p, mv_p, None, None, conv0, s_zero)

    bs = x_sample.shape[0]
    ys, s_ak, s_av, s_cv, s_sb = _trunk(
        x_sample, lp, cache_mem_k[l].reshape(bs, n_mem, d), cache_mem_v[l].reshape(bs, n_mem, d),
        cache_a_k[l], cache_a_v[l], state_b_conv[l], state_b_s[l])

    mem_heads = (1, bp, n_mem, N_HEADS_X, d // N_HEADS_X)
    return (yp, ys, p_ak[None], p_av[None], p_cv[None], p_sb[None],
            mk_p.reshape(mem_heads), mv_p.reshape(mem_heads),
            s_ak[None], s_av[None], s_cv[None], s_sb[None])
```

```python
import functools

import jax
import jax.numpy as jnp
from jax import lax
from jax.experimental import pallas as pl
from jax.experimental.pallas import tpu as pltpu

F32 = jnp.float32
BF16 = jnp.bfloat16
EPS = 1e-6

CHUNK = 64
A_REACH = 8 * CHUNK
REL_CLIP = 128
N_HEADS_A = 8
HEAD_DIM_A = 64
WIDTH_A = N_HEADS_A * HEAD_DIM_A
N_HEADS_B = 4
HEAD_DIM_B = 128
WIDTH_B = N_HEADS_B * HEAD_DIM_B
CONV_W = 4
B_CONV_DIM = 3 * WIDTH_B
N_HEADS_X = 4
LANES = 128

VMEM_LIMIT_BYTES = 56 * 1024 * 1024


def _params(n_grid_axes):
    return pltpu.CompilerParams(
        dimension_semantics=("arbitrary",) * n_grid_axes,
        vmem_limit_bytes=VMEM_LIMIT_BYTES,
    )


def _resident(shape):
    nd = len(shape)
    return pl.BlockSpec(shape, lambda *_: (0,) * nd, pipeline_mode=pl.Buffered(1))


def _rms(x, g):
    return x * lax.rsqrt(jnp.mean(x * x, axis=-1, keepdims=True) + EPS) * g


def _silu(x):
    return x * jax.nn.sigmoid(x)


def _softplus(x):
    return jnp.maximum(x, 0.0) + jnp.log1p(jnp.exp(-jnp.abs(x)))


def _dot(a, b):
    return jnp.dot(a, b, preferred_element_type=F32)


def _dot_nt(a, b):
    return lax.dot_general(a, b, (((1,), (1,)), ((), ())), preferred_element_type=F32)


def _split2(x):
    hi = x.astype(BF16)
    lo = (x - hi.astype(F32)).astype(BF16)
    return hi, lo


def _token_tile(n, want):
    t = min(want, n)
    while n % t:
        t //= 2
    return t


def _ffn_body(*refs, final):
    if final:
        x_ref, g_ref, wg_ref, wu_ref, wd_ref, fg_ref, o_ref = refs
    else:
        x_ref, g_ref, wg_ref, wu_ref, wd_ref, o_ref = refs
    x = x_ref[...]
    h = _rms(x, g_ref[...]).astype(BF16)
    gate = _dot(h, wg_ref[...])
    up = _dot(h, wu_ref[...])
    a = (_silu(gate) * up).astype(BF16)
    y = x + 0.5 * _dot(a, wd_ref[...])
    if final:
        y = _rms(y, fg_ref[...])
    o_ref[...] = y


def _ffn(x, g, wg, wu, wd, final_g=None, tile=512):
    n, d = x.shape
    f = wg.shape[1]
    tm = _token_tile(n, tile)
    final = final_g is not None
    in_specs = [
        pl.BlockSpec((tm, d), lambda i: (i, 0)),
        _resident((1, d)),
        _resident((d, f)),
        _resident((d, f)),
        _resident((f, d)),
    ]
    args = [x, g, wg, wu, wd]
    if final:
        in_specs.append(_resident((1, d)))
        args.append(final_g)
    return pl.pallas_call(
        functools.partial(_ffn_body, final=final),
        out_shape=jax.ShapeDtypeStruct((n, d), F32),
        grid=(n // tm,),
        in_specs=in_specs,
        out_specs=pl.BlockSpec((tm, d), lambda i: (i, 0)),
        compiler_params=_params(1),
        name="ffn_final" if final else "ffn",
    )(*args)


_PROJ_WIDTHS = (WIDTH_A, WIDTH_A, WIDTH_A, B_CONV_DIM, WIDTH_B, None, LANES)


def _proj_body(x_ref, g_ref, w_ref, *out_refs, widths):
    h = _rms(x_ref[...], g_ref[...]).astype(BF16)
    off = 0
    for o_ref, wdt in zip(out_refs, widths):
        o_ref[...] = _dot(h, w_ref[:, off:off + wdt])
        off += wdt


def _proj(x, g, w_cat, tile=512):
    n, d = x.shape
    widths = tuple(2 * d if w is None else w for w in _PROJ_WIDTHS)
    assert sum(widths) == w_cat.shape[1]
    tm = _token_tile(n, tile)
    return pl.pallas_call(
        functools.partial(_proj_body, widths=widths),
        out_shape=tuple(jax.ShapeDtypeStruct((n, w), F32) for w in widths),
        grid=(n // tm,),
        in_specs=[
            pl.BlockSpec((tm, d), lambda i: (i, 0)),
            _resident((1, d)),
            _resident(w_cat.shape),
        ],
        out_specs=tuple(pl.BlockSpec((tm, w), lambda i: (i, 0)) for w in widths),
        compiler_params=_params(1),
        name="in_proj",
    )(x, g, w_cat)


def _pair_masks(rows):
    lane = lax.broadcasted_iota(jnp.int32, (rows, LANES), 1)
    return lane < HEAD_DIM_A


Q_TILE = 2 * CHUNK
K_TILE = A_REACH + Q_TILE
LOG2E = 1.4426950408889634


def _attn_prompt_body(q_ref, k_ref, v_ref, bias_ref, o_ref, kpad, vpad):
    s_len = q_ref.shape[1]
    n_pairs = N_HEADS_A // 2
    qscale = HEAD_DIM_A ** -0.5 * LOG2E

    kpad[0:A_REACH, :] = jnp.zeros((A_REACH, WIDTH_A), BF16)
    vpad[0:A_REACH, :] = jnp.zeros((A_REACH, WIDTH_A), BF16)

    def fill(i, carry):
        r = pl.multiple_of(i * 256, 256)
        kpad[pl.ds(A_REACH + r, 256), :] = k_ref[0, pl.ds(r, 256), :].astype(BF16)
        vpad[pl.ds(A_REACH + r, 256), :] = v_ref[0, pl.ds(r, 256), :].astype(BF16)
        return carry

    lax.fori_loop(0, s_len // 256, fill, 0)

    low = _pair_masks(Q_TILE)
    kcol = lax.broadcasted_iota(jnp.int32, (2 * Q_TILE, K_TILE), 1)

    def step(it, carry):
        start = pl.multiple_of(it * Q_TILE, Q_TILE)
        exists = (kcol + start) >= A_REACH
        scores = []
        for hp in range(n_pairs):
            sl = slice(hp * LANES, (hp + 1) * LANES)
            qp = q_ref[0, pl.ds(start, Q_TILE), sl] * qscale
            lhs = jnp.concatenate([jnp.where(low, qp, 0.0), jnp.where(low, 0.0, qp)], axis=0)
            scores.append(_dot_nt(lhs.astype(BF16), kpad[pl.ds(start, K_TILE), sl]))
        probs, inv = [], []
        for hp in range(n_pairs):
            s = jnp.where(exists, scores[hp] + bias_ref[hp], -jnp.inf)
            m = jnp.max(s, axis=-1, keepdims=True)
            p = jnp.exp2(s - m)
            inv.append(1.0 / jnp.sum(p, axis=-1, keepdims=True))
            probs.append(p.astype(BF16))
        for hp in range(n_pairs):
            sl = slice(hp * LANES, (hp + 1) * LANES)
            o = _dot(probs[hp], vpad[pl.ds(start, K_TILE), sl]) * inv[hp]
            o_ref[0, pl.ds(start, Q_TILE), sl] = jnp.where(low, o[:Q_TILE], o[Q_TILE:])
        return carry

    lax.fori_loop(0, s_len // Q_TILE, step, 0)


def _attn_prompt(q, k, v, bias):
    b, s_len, w = q.shape
    assert s_len % 256 == 0
    seq = pl.BlockSpec((1, s_len, w), lambda i: (i, 0, 0))
    return pl.pallas_call(
        _attn_prompt_body,
        out_shape=jax.ShapeDtypeStruct((b, s_len, w), F32),
        grid=(b,),
        in_specs=[seq, seq, seq, _resident(bias.shape)],
        out_specs=seq,
        scratch_shapes=[
            pltpu.VMEM((A_REACH + s_len, w), BF16),
            pltpu.VMEM((A_REACH + s_len, w), BF16),
        ],
        compiler_params=_params(1),
        name="band_attn_prompt",
    )(q, k, v, bias)


def _attn_sample_body(q_ref, k_ref, v_ref, ck_ref, cv_ref, bc_ref, bn_ref, o_ref):
    t = q_ref.shape[1]
    scale = HEAD_DIM_A ** -0.5
    low = _pair_masks(t)
    for hp in range(N_HEADS_A // 2):
        sl = slice(hp * LANES, (hp + 1) * LANES)
        qp = q_ref[0, :, sl] * scale
        kn = k_ref[0, :, sl].astype(BF16)
        vn = v_ref[0, :, sl].astype(BF16)
        kc = ck_ref[0, :, sl].astype(BF16)
        vc = cv_ref[0, :, sl].astype(BF16)
        outs = []
        for half in range(2):
            h = 2 * hp + half
            qm = jnp.where(low if half == 0 else ~low, qp, 0.0).astype(BF16)
            sc = _dot_nt(qm, kc) + bc_ref[h]
            sn = _dot_nt(qm, kn) + bn_ref[h]
            m = jnp.maximum(jnp.max(sc, axis=-1, keepdims=True), jnp.max(sn, axis=-1, keepdims=True))
            pc = jnp.exp(sc - m)
            pn = jnp.exp(sn - m)
            inv = 1.0 / (jnp.sum(pc, axis=-1, keepdims=True) + jnp.sum(pn, axis=-1, keepdims=True))
            outs.append(_dot((pc * inv).astype(BF16), vc) + _dot((pn * inv).astype(BF16), vn))
        o_ref[0, :, sl] = jnp.where(low, outs[0], outs[1])


def _attn_sample(q, k, v, ck, cv, bias_cache, bias_new):
    b, t, w = q.shape
    p_len = ck.shape[1]
    new = pl.BlockSpec((1, t, w), lambda i: (i, 0, 0))
    old = pl.BlockSpec((1, p_len, w), lambda i: (i, 0, 0))
    return pl.pallas_call(
        _attn_sample_body,
        out_shape=jax.ShapeDtypeStruct((b, t, w), F32),
        grid=(b,),
        in_specs=[new, new, new, old, old, _resident(bias_cache.shape), _resident(bias_new.shape)],
        out_specs=new,
        compiler_params=_params(1),
        name="band_attn_sample",
    )(q, k, v, ck, cv, bias_cache, bias_new)


DELTA_BLOCK = 512
DELTA_GROUP = 4
HIST_ROWS = 8


def _lane_blocks(a, b):
    z = jnp.zeros_like(a)
    return jnp.concatenate([jnp.concatenate([a, z], axis=-1), jnp.concatenate([z, b], axis=-1)], axis=0)


def _delta_body(xb_ref, bd_ref, prev_ref, s0_ref, cw_ref, alog_ref, dtb_ref,
                o_ref, conv_ref, s_ref,
                hist, s_scr, u_scr, wq_scr, ik_scr, gl_scr, *, valid_len, last_rows, group):
    j = pl.program_id(1)
    tb = xb_ref.shape[1]
    L = CHUNK
    hd = HEAD_DIM_B
    assert hd == 2 * L and N_HEADS_B == 4
    n_chunks = tb // L
    n_pairs = N_HEADS_B // 2

    @pl.when(j == 0)
    def _():
        hist[...] = prev_ref[0]
        for p in range(n_pairs):
            s_scr[p] = jnp.concatenate([s0_ref[0, 2 * p], s0_ref[0, 2 * p + 1]], axis=-1)

    row = lax.broadcasted_iota(jnp.int32, (L, 2 * L), 0)
    lane = lax.broadcasted_iota(jnp.int32, (L, 2 * L), 1)
    col = lane & (L - 1)
    low = lane < L
    incl = col <= row
    strict = col < row
    eye = jnp.where(col == row, 1.0, 0.0).astype(F32)
    n_levels = L.bit_length() - 1
    level = [((row >> (k + 1)) == (col >> (k + 1))) & (((row >> k) & 1) == 1) & (((col >> k) & 1) == 0)
             for k in range(n_levels)]
    r64 = lax.broadcasted_iota(jnp.int32, (L, L), 0)
    c64 = lax.broadcasted_iota(jnp.int32, (L, L), 1)
    tril = jnp.where(c64 <= r64, 1.0, 0.0).astype(BF16)
    neg_a = -jnp.exp(alog_ref[...])
    dtb = dtb_ref[...]
    cw = [cw_ref[i:i + 1, :] for i in range(CONV_W)]
    zero_t0 = jnp.zeros((HIST_ROWS, 1), jnp.int32)
    tok = lax.broadcasted_iota(jnp.int32, (L, 1), 0)

    def bd_pair(y):
        return jnp.concatenate([jnp.where(low, y, 0.0), jnp.where(low, 0.0, y)], axis=0)

    def pair_prod(x, y):
        xh, xl = _split2(x)
        yh, yl = _split2(bd_pair(y))
        return _dot(xh, yh) + _dot(xh, yl) + _dot(xl, yh)

    def prep_chunk(c):
        t0 = pl.multiple_of(c * L, L)
        cur = xb_ref[0, pl.ds(t0, L), :]
        before = xb_ref[0, pl.ds(pl.multiple_of(jnp.maximum(t0 - HIST_ROWS, 0), HIST_ROWS), HIST_ROWS), :]
        before = jnp.where((zero_t0 + t0) == 0, hist[...], before)
        win = jnp.concatenate([before, cur], axis=0)
        y = win[HIST_ROWS:HIST_ROWS + L] * cw[CONV_W - 1]
        for i in range(CONV_W - 1):
            lo = HIST_ROWS - (CONV_W - 1) + i
            y = y + win[lo:lo + L] * cw[i]
        xc = _silu(y)
        bdc = bd_ref[0, pl.ds(t0, L), :]
        beta_all = jax.nn.sigmoid(bdc)
        g_all = neg_a * _softplus(bdc + dtb)
        if valid_len is not None:
            ok = (tok + (j * tb + t0)) < valid_len
            beta_all = jnp.where(ok, beta_all, 0.0)
            g_all = jnp.where(ok, g_all, 0.0)
        g1 = g_all.astype(BF16)
        r1 = g_all - g1.astype(F32)
        g2 = r1.astype(BF16)
        g3 = (r1 - g2.astype(F32)).astype(BF16)
        gc_all = _dot(tril, g1) + _dot(tril, g2) + _dot(tril, g3)
        heads = []
        for h in range(N_HEADS_B):
            q = xc[:, h * hd:(h + 1) * hd]
            k = xc[:, WIDTH_B + h * hd:WIDTH_B + (h + 1) * hd]
            v = xc[:, 2 * WIDTH_B + h * hd:2 * WIDTH_B + (h + 1) * hd]
            q = q * lax.rsqrt(jnp.sum(q * q, axis=-1, keepdims=True) + EPS) * (hd ** -0.5)
            k = k * lax.rsqrt(jnp.sum(k * k, axis=-1, keepdims=True) + EPS)
            beta = jnp.broadcast_to(beta_all[:, h:h + 1], (L, hd))
            gc = jnp.broadcast_to(gc_all[:, N_HEADS_B + h:N_HEADS_B + h + 1], (L, hd))
            eg = jnp.exp(gc)
            g_last = gc[L - 1:L, :]
            kbeta = k * beta
            heads.append(dict(q=q, k=k, kbeta=kbeta, vbeta=v * beta, kbeg=kbeta * eg, qg=q * eg,
                              kdec=k * jnp.exp(g_last - gc), gc=gc, eg_last=jnp.exp(g_last)))
        return heads

    def phase1(gi, carry):
        chunks = [gi * group + g for g in range(group)]
        heads = [prep_chunk(c) for c in chunks]
        units = [(g, p) for g in range(group) for p in range(n_pairs)]
        gram = {}
        for (g, p) in units:
            a, b = heads[g][2 * p], heads[g][2 * p + 1]
            rhs = _lane_blocks(a["k"], b["k"]).astype(BF16)
            lhs = jnp.concatenate([jnp.concatenate([a["kbeta"], b["kbeta"]], axis=-1),
                                   jnp.concatenate([a["q"], b["q"]], axis=-1)], axis=0).astype(BF16)
            gram[g, p] = _dot_nt(lhs, rhs)
        a_mat, intra, t_inv = {}, {}, {}
        for (g, p) in units:
            a, b = heads[g][2 * p], heads[g][2 * p + 1]
            gc_i = jnp.where(low, a["gc"], b["gc"])
            gc_j = jnp.concatenate([a["gc"], b["gc"]], axis=0).T[0:L, :]
            decay = jnp.exp(jnp.where(incl, gc_i - gc_j, -jnp.inf))
            a_mat[g, p] = jnp.where(strict, gram[g, p][0:L] * decay, 0.0)
            intra[g, p] = gram[g, p][L:2 * L] * decay
            t_inv[g, p] = eye - jnp.where(level[0], a_mat[g, p], 0.0)
        for k in range(1, n_levels):
            y = {u: pair_prod(jnp.where(level[k], a_mat[u], 0.0), t_inv[u]) for u in units}
            t_inv = {u: t_inv[u] - pair_prod(t_inv[u], y[u]) for u in units}

        def solve_rhs(x, y):
            z = jnp.zeros((L, hd), BF16)
            return jnp.concatenate([jnp.concatenate([x[0:L], z, x[L:2 * L], z], axis=-1),
                                    jnp.concatenate([z, y[0:L], z, y[L:2 * L]], axis=-1)], axis=0)

        for (g, p) in units:
            a, b = heads[g][2 * p], heads[g][2 * p + 1]
            th, tl = _split2(t_inv[g, p])
            ah, al = _split2(jnp.concatenate([a["vbeta"], a["kbeg"]], axis=0))
            bh, bl = _split2(jnp.concatenate([b["vbeta"], b["kbeg"]], axis=0))
            rh, rl = solve_rhs(ah, bh), solve_rhs(al, bl)
            sol = _dot(th, rh) + _dot(th, rl) + _dot(tl, rh)
            c = chunks[g]
            u_scr[c, p] = sol[:, 0:2 * hd]
            wq_scr[c, p] = jnp.concatenate(
                [sol[:, 2 * hd:4 * hd], jnp.concatenate([a["qg"], b["qg"]], axis=-1)], axis=0).astype(BF16)
            kdec_t = jnp.concatenate([a["kdec"], b["kdec"]], axis=0).T
            ik_scr[c, p] = jnp.concatenate([intra[g, p], kdec_t], axis=0).astype(BF16)
            gl_scr[c, p] = jnp.broadcast_to(
                jnp.concatenate([a["eg_last"], b["eg_last"]], axis=-1), (HIST_ROWS, 2 * hd))
        return carry

    lax.fori_loop(0, n_chunks // group, phase1, 0)

    def phase2(c, carry):
        t0 = pl.multiple_of(c * L, L)
        r1, r2 = [], []
        for p in range(n_pairs):
            s = s_scr[p]
            r1.append(_dot(wq_scr[c, p], _lane_blocks(s[:, 0:hd], s[:, hd:2 * hd]).astype(BF16)))
        for p in range(n_pairs):
            v_new = u_scr[c, p] - r1[p][0:L]
            r2.append(_dot(ik_scr[c, p], _lane_blocks(v_new[:, 0:hd], v_new[:, hd:2 * hd]).astype(BF16)))
        for p in range(n_pairs):
            o_ref[0, pl.ds(t0, L), p * 2 * hd:(p + 1) * 2 * hd] = r1[p][L:2 * L] + r2[p][0:L]
            s_scr[p] = s_scr[p] * gl_scr[c, p][0:1, :] + r2[p][L:L + hd]
        return carry

    lax.fori_loop(0, n_chunks, phase2, 0)

    hist[...] = xb_ref[0, tb - HIST_ROWS:tb, :]

    @pl.when(j == pl.num_programs(1) - 1)
    def _():
        conv_ref[0] = xb_ref[0, last_rows - (CONV_W - 1):last_rows, :]
        for h in range(N_HEADS_B):
            s_ref[0, h] = s_scr[h // 2][:, (h % 2) * hd:(h % 2 + 1) * hd]


def _delta(xb, bd, prev, s0, conv_w, alog_row, dtb_row):
    b, t, cdim = xb.shape
    valid_len = None
    if t % CHUNK:
        assert CONV_W - 1 <= t < CHUNK
        valid_len = t
        xb = jnp.pad(xb, ((0, 0), (0, CHUNK - t), (0, 0)))
        bd = jnp.pad(bd, ((0, 0), (0, CHUNK - t), (0, 0)))
    tp = xb.shape[1]
    tb = _token_tile(tp, DELTA_BLOCK)
    n_chunks = tb // CHUNK
    group = _token_tile(n_chunks, DELTA_GROUP)
    last_rows = tb if valid_len is None else valid_len
    prev = jnp.pad(prev, ((0, 0), (HIST_ROWS - (CONV_W - 1), 0), (0, 0)))
    state = pl.BlockSpec((1, N_HEADS_B, HEAD_DIM_B, HEAD_DIM_B), lambda i, j: (i, 0, 0, 0))
    n_pairs = N_HEADS_B // 2
    o, conv, s = pl.pallas_call(
        functools.partial(_delta_body, valid_len=valid_len, last_rows=last_rows, group=group),
        out_shape=(
            jax.ShapeDtypeStruct((b, tp, WIDTH_B), F32),
            jax.ShapeDtypeStruct((b, CONV_W - 1, cdim), F32),
            jax.ShapeDtypeStruct((b, N_HEADS_B, HEAD_DIM_B, HEAD_DIM_B), F32),
        ),
        grid=(b, tp // tb),
        in_specs=[
            pl.BlockSpec((1, tb, cdim), lambda i, j: (i, j, 0)),
            pl.BlockSpec((1, tb, LANES), lambda i, j: (i, j, 0)),
            pl.BlockSpec((1, HIST_ROWS, cdim), lambda i, j: (i, 0, 0)),
            state,
            _resident((CONV_W, cdim)),
            _resident((1, LANES)),
            _resident((1, LANES)),
        ],
        out_specs=(
            pl.BlockSpec((1, tb, WIDTH_B), lambda i, j: (i, j, 0)),
            pl.BlockSpec((1, CONV_W - 1, cdim), lambda i, j: (i, 0, 0)),
            state,
        ),
        scratch_shapes=[
            pltpu.VMEM((HIST_ROWS, cdim), F32),
            pltpu.VMEM((n_pairs, HEAD_DIM_B, 2 * HEAD_DIM_B), F32),
            pltpu.VMEM((n_chunks, n_pairs, CHUNK, 2 * HEAD_DIM_B), F32),
            pltpu.VMEM((n_chunks, n_pairs, 2 * CHUNK, 2 * HEAD_DIM_B), BF16),
            pltpu.VMEM((n_chunks, n_pairs, CHUNK + HEAD_DIM_B, 2 * CHUNK), BF16),
            pltpu.VMEM((n_chunks, n_pairs, HIST_ROWS, 2 * HEAD_DIM_B), F32),
        ],
        compiler_params=_params(2),
        name="gated_delta",
    )(xb, bd, prev, s0, conv_w, alog_row, dtb_row)
    return o[:, :t], conv, s


def _merge_body(x_ref, ya_ref, ob_ref, z_ref, gate_ref, on_ref, wa_ref, wb_ref, wm_ref, o_ref):
    d = x_ref.shape[1]
    hd = HEAD_DIM_B
    ob = ob_ref[...]
    z = z_ref[...]
    parts = []
    for h in range(N_HEADS_B):
        oh = ob[:, h * hd:(h + 1) * hd]
        parts.append(_rms(oh, on_ref[...]) * _silu(z[:, h * hd:(h + 1) * hd]))
    obn = jnp.concatenate(parts, axis=-1).astype(BF16)
    gate = jax.nn.sigmoid(gate_ref[...])
    merged = (gate[:, :d] * _dot(ya_ref[...].astype(BF16), wa_ref[...])
              + gate[:, d:] * _dot(obn, wb_ref[...]))
    o_ref[...] = x_ref[...] + _dot(merged.astype(BF16), wm_ref[...])


def _merge(x, ya, ob, z, gates, out_norm, wa, wb, wm, tile=512):
    n, d = x.shape
    tm = _token_tile(n, tile)

    def tok(w):
        return pl.BlockSpec((tm, w), lambda i: (i, 0))

    return pl.pallas_call(
        _merge_body,
        out_shape=jax.ShapeDtypeStruct((n, d), F32),
        grid=(n // tm,),
        in_specs=[tok(d), tok(WIDTH_A), tok(WIDTH_B), tok(WIDTH_B), tok(2 * d),
                  _resident((1, HEAD_DIM_B)), _resident(wa.shape), _resident(wb.shape), _resident(wm.shape)],
        out_specs=tok(d),
        compiler_params=_params(1),
        name="merge",
    )(x, ya, ob, z, gates, out_norm, wa, wb, wm)


def _memkv_body(m_ref, g_ref, wk_ref, wv_ref, k_ref, v_ref):
    h = _rms(m_ref[...], g_ref[...]).astype(BF16)
    k_ref[...] = _dot(h, wk_ref[...])
    v_ref[...] = _dot(h, wv_ref[...])


def _memkv(mem, g, wk, wv, tile=512):
    n, d = mem.shape
    tm = _token_tile(n, tile)
    tok = pl.BlockSpec((tm, d), lambda i: (i, 0))
    return pl.pallas_call(
        _memkv_body,
        out_shape=(jax.ShapeDtypeStruct((n, d), F32),) * 2,
        grid=(n // tm,),
        in_specs=[tok, _resident((1, d)), _resident(wk.shape), _resident(wv.shape)],
        out_specs=(tok, tok),
        compiler_params=_params(1),
        name="mem_kv",
    )(mem, g, wk, wv)


def _xattn_body(x_ref, mk_ref, mv_ref, g_ref, wq_ref, wo_ref, o_ref):
    x = x_ref[0]
    d = x.shape[1]
    hd = d // N_HEADS_X
    q = _dot(_rms(x, g_ref[...]).astype(BF16), wq_ref[...]) * (hd ** -0.5)
    outs = []
    for h in range(N_HEADS_X):
        sl = slice(h * hd, (h + 1) * hd)
        s = _dot_nt(q[:, sl].astype(BF16), mk_ref[0, :, sl].astype(BF16))
        m = jnp.max(s, axis=-1, keepdims=True)
        p = jnp.exp(s - m)
        p = p * (1.0 / jnp.sum(p, axis=-1, keepdims=True))
        outs.append(_dot(p.astype(BF16), mv_ref[0, :, sl].astype(BF16)))
    o = jnp.concatenate(outs, axis=-1).astype(BF16)
    o_ref[0] = x + _dot(o, wo_ref[...])


def _xattn(x, mk, mv, g, wq, wo, tile=512):
    b, t, d = x.shape
    n_mem = mk.shape[1]
    tq = _token_tile(t, tile)
    tok = pl.BlockSpec((1, tq, d), lambda i, j: (i, j, 0))
    mem = pl.BlockSpec((1, n_mem, d), lambda i, j: (i, 0, 0))
    return pl.pallas_call(
        _xattn_body,
        out_shape=jax.ShapeDtypeStruct((b, t, d), F32),
        grid=(b, t // tq),
        in_specs=[tok, mem, mem, _resident((1, d)), _resident(wq.shape), _resident(wo.shape)],
        out_specs=tok,
        compiler_params=_params(2),
        name="cross_attn",
    )(x, mk, mv, g, wq, wo)


def _rel_bias(table, q_len, k_pos):
    d = jnp.arange(q_len)[:, None] - k_pos[None, :]
    return table[:, jnp.clip(d, -REL_CLIP, REL_CLIP) + REL_CLIP].astype(F32)


def _band_bias(table):
    t = jnp.arange(Q_TILE)
    kk = jnp.arange(K_TILE)
    bias = _rel_bias(table, Q_TILE, kk - A_REACH) * LOG2E
    q_chunk = t[:, None] // CHUNK
    k_chunk = kk[None, :] // CHUNK
    in_band = (k_chunk >= q_chunk) & (k_chunk <= q_chunk + A_REACH // CHUNK)
    bias = jnp.where(in_band[None], bias, -jnp.inf)
    return bias.reshape(N_HEADS_A // 2, 2 * Q_TILE, K_TILE)


def _trunk(x, lp, mem_k, mem_v, a_k_cache, a_v_cache, conv_prev, s0):
    b, t, d = x.shape
    n = b * t
    x = _ffn(x.reshape(n, d), lp["ffn1_norm"], lp["ffn1_wg"], lp["ffn1_wu"], lp["ffn1_wd"])
    qa, ka, va, xb, z, gates, bd = _proj(x, lp["mix_norm"], lp["w_cat"])
    qa, ka, va = (a.reshape(b, t, WIDTH_A) for a in (qa, ka, va))
    if a_k_cache is None:
        ya = _attn_prompt(qa, ka, va, _band_bias(lp["a_rel_bias"]))
        keep = min(A_REACH, t)
        new_k, new_v = ka[:, t - keep:], va[:, t - keep:]
    else:
        p_len = a_k_cache.shape[1]
        ya = _attn_sample(
            qa, ka, va, a_k_cache.reshape(b, p_len, WIDTH_A), a_v_cache.reshape(b, p_len, WIDTH_A),
            _rel_bias(lp["a_rel_bias"], t, jnp.arange(-p_len, 0)),
            _rel_bias(lp["a_rel_bias"], t, jnp.arange(t)))
        new_k, new_v = ka, va
    ob, new_conv, new_s = _delta(
        xb.reshape(b, t, B_CONV_DIM), bd.reshape(b, t, LANES), conv_prev, s0,
        lp["b_conv_w"], lp["alog_row"], lp["dtb_row"])
    x = _merge(x, ya.reshape(n, WIDTH_A), ob.reshape(n, WIDTH_B), z, gates, lp["b_out_norm"],
               lp["w_branch_a"], lp["w_branch_b"], lp["w_mix_out"])
    x = _xattn(x.reshape(b, t, d), mem_k, mem_v, lp["xattn_norm"], lp["xattn_wq"], lp["xattn_wo"])
    y = _ffn(x.reshape(n, d), lp["ffn2_norm"], lp["ffn2_wg"], lp["ffn2_wu"], lp["ffn2_wd"],
             final_g=lp["final_norm"])
    heads = (b, -1, N_HEADS_A, HEAD_DIM_A)
    return y.reshape(b, t, d), new_k.reshape(heads), new_v.reshape(heads), new_conv, new_s


def kernel(x_prompt, x_sample, cache_a_k, cache_a_v, state_b_conv, state_b_s, cache_mem_k, cache_mem_v, mem_prompt, ffn1_norm, ffn1_w_gate, ffn1_w_up, ffn1_w_down, mix_norm, w_in, a_rel_bias, b_conv_w, b_a_log, b_dt_bias, b_out_norm, w_branch_a, w_branch_b, w_mix_out, xattn_norm, mem_norm, xattn_wq, xattn_wk, xattn_wv, xattn_wo, ffn2_norm, ffn2_w_gate, ffn2_w_up, ffn2_w_down, final_norm):
    depth = ffn1_norm.shape[0]
    assert depth == 1
    l = 0
    d = x_prompt.shape[-1]
    bp = x_prompt.shape[0]
    n_mem = mem_prompt.shape[1]

    w = w_in[l]
    off_b = 3 * WIDTH_A
    off_z = off_b + B_CONV_DIM
    off_beta = off_z + WIDTH_B
    off_gate = off_beta + 2 * N_HEADS_B
    small = jnp.pad(w[:, off_beta:off_gate], ((0, 0), (0, LANES - 2 * N_HEADS_B)))
    w_cat = jnp.concatenate([w[:, :off_beta], w[:, off_gate:], small], axis=1).astype(BF16)

    def lane_row(vec):
        return jnp.pad(vec.astype(F32), (N_HEADS_B, LANES - 2 * N_HEADS_B)).reshape(1, LANES)

    def row(vec):
        return vec.astype(F32).reshape(1, -1)

    lp = {
        "ffn1_norm": row(ffn1_norm[l]), "ffn1_wg": ffn1_w_gate[l].astype(BF16),
        "ffn1_wu": ffn1_w_up[l].astype(BF16), "ffn1_wd": ffn1_w_down[l].astype(BF16),
        "mix_norm": row(mix_norm[l]), "w_cat": w_cat, "a_rel_bias": a_rel_bias[l],
        "b_conv_w": b_conv_w[l], "alog_row": lane_row(b_a_log[l]), "dtb_row": lane_row(b_dt_bias[l]),
        "b_out_norm": row(b_out_norm[l]), "w_branch_a": w_branch_a[l].astype(BF16),
        "w_branch_b": w_branch_b[l].astype(BF16), "w_mix_out": w_mix_out[l].astype(BF16),
        "xattn_norm": row(xattn_norm[l]), "xattn_wq": xattn_wq[l].astype(BF16),
        "xattn_wo": xattn_wo[l].astype(BF16), "ffn2_norm": row(ffn2_norm[l]),
        "ffn2_wg": ffn2_w_gate[l].astype(BF16), "ffn2_wu": ffn2_w_up[l].astype(BF16),
        "ffn2_wd": ffn2_w_down[l].astype(BF16), "final_norm": row(final_norm),
    }

    mk_p, mv_p = _memkv(mem_prompt.reshape(bp * n_mem, d), row(mem_norm[l]),
                        xattn_wk[l].astype(BF16), xattn_wv[l].astype(BF16))
    mk_p = mk_p.reshape(bp, n_mem, d)
    mv_p = mv_p.reshape(bp, n_mem, d)
    conv0 = jnp.zeros((bp, CONV_W - 1, B_CONV_DIM), F32)
    s_zero = jnp.zeros((bp, N_HEADS_B, HEAD_DIM_B, HEAD_DIM_B), F32)
    yp, p_ak, p_av, p_cv, p_sb = _trunk(x_prompt, lp, mk_p, mv_p, None, None, conv0, s_zero)

    bs = x_sample.shape[0]
    ys, s_ak, s_av, s_cv, s_sb = _trunk(
        x_sample, lp, cache_mem_k[l].reshape(bs, n_mem, d), cache_mem_v[l].reshape(bs, n_mem, d),
        cache_a_k[l], cache_a_v[l], state_b_conv[l], state_b_s[l])

    mem_heads = (1, bp, n_mem, N_HEADS_X, d // N_HEADS_X)
    return (yp, ys, p_ak[None], p_av[None], p_cv[None], p_sb[None],
            mk_p.reshape(mem_heads), mv_p.reshape(mem_heads),
            s_ak[None], s_av[None], s_cv[None], s_sb[None])
```

```python
import functools

import jax
import jax.numpy as jnp
from jax import lax
from jax.experimental import pallas as pl
from jax.experimental.pallas import tpu as pltpu

F32 = jnp.float32
BF16 = jnp.bfloat16
EPS = 1e-6

CHUNK = 64
A_REACH = 8 * CHUNK
REL_CLIP = 128
N_HEADS_A = 8
HEAD_DIM_A = 64
WIDTH_A = N_HEADS_A * HEAD_DIM_A
N_HEADS_B = 4
HEAD_DIM_B = 128
WIDTH_B = N_HEADS_B * HEAD_DIM_B
CONV_W = 4
B_CONV_DIM = 3 * WIDTH_B
N_HEADS_X = 4
LANES = 128

VMEM_LIMIT_BYTES = 56 * 1024 * 1024


def _params(n_grid_axes):
    return pltpu.CompilerParams(
        dimension_semantics=("arbitrary",) * n_grid_axes,
        vmem_limit_bytes=VMEM_LIMIT_BYTES,
    )


def _resident(shape):
    nd = len(shape)
    return pl.BlockSpec(shape, lambda *_: (0,) * nd, pipeline_mode=pl.Buffered(1))


def _rms(x, g):
    return x * lax.rsqrt(jnp.mean(x * x, axis=-1, keepdims=True) + EPS) * g


def _silu(x):
    return x * jax.nn.sigmoid(x)


def _softplus(x):
    return jnp.maximum(x, 0.0) + jnp.log1p(jnp.exp(-jnp.abs(x)))


def _dot(a, b):
    return jnp.dot(a, b, preferred_element_type=F32)


def _dot_nt(a, b):
    return lax.dot_general(a, b, (((1,), (1,)), ((), ())), preferred_element_type=F32)


def _split2(x):
    hi = x.astype(BF16)
    lo = (x - hi.astype(F32)).astype(BF16)
    return hi, lo


def _token_tile(n, want):
    t = min(want, n)
    while n % t:
        t //= 2
    return t


def _ffn_body(*refs, final):
    if final:
        x_ref, g_ref, wg_ref, wu_ref, wd_ref, fg_ref, o_ref = refs
    else:
        x_ref, g_ref, wg_ref, wu_ref, wd_ref, o_ref = refs
    x = x_ref[...]
    h = _rms(x, g_ref[...]).astype(BF16)
    gate = _dot(h, wg_ref[...])
    up = _dot(h, wu_ref[...])
    a = (_silu(gate) * up).astype(BF16)
    y = x + 0.5 * _dot(a, wd_ref[...])
    if final:
        y = _rms(y, fg_ref[...])
    o_ref[...] = y


def _ffn(x, g, wg, wu, wd, final_g=None, tile=512):
    n, d = x.shape
    f = wg.shape[1]
    tm = _token_tile(n, tile)
    final = final_g is not None
    in_specs = [
        pl.BlockSpec((tm, d), lambda i: (i, 0)),
        _resident((1, d)),
        _resident((d, f)),
        _resident((d, f)),
        _resident((f, d)),
    ]
    args = [x, g, wg, wu, wd]
    if final:
        in_specs.append(_resident((1, d)))
        args.append(final_g)
    return pl.pallas_call(
        functools.partial(_ffn_body, final=final),
        out_shape=jax.ShapeDtypeStruct((n, d), F32),
        grid=(n // tm,),
        in_specs=in_specs,
        out_specs=pl.BlockSpec((tm, d), lambda i: (i, 0)),
        compiler_params=_params(1),
        name="ffn_final" if final else "ffn",
    )(*args)


LOG2E = 1.4426950408889634
Q_SCALE_A = HEAD_DIM_A ** -0.5 * LOG2E
HIST_ROWS = 8
CONV_ROWS = 64
_PROJ_WIDTHS = (WIDTH_A, WIDTH_A, WIDTH_A, B_CONV_DIM, LANES)


def _causal_conv_silu(before, cur, cw_ref):
    n = cur.shape[0]
    win = jnp.concatenate([before, cur], axis=0)
    y = cur * cw_ref[CONV_W - 1:CONV_W, :]
    for i in range(CONV_W - 1):
        lo = HIST_ROWS - (CONV_W - 1) + i
        y = y + win[lo:lo + n] * cw_ref[i:i + 1, :]
    return _silu(y)


def _proj_body(*refs, steps_per_seq):
    if steps_per_seq is None:
        x_ref, g_ref, w_ref, q_ref, k_ref, v_ref, xb_ref, bd_ref = refs
    else:
        x_ref, g_ref, w_ref, cw_ref, q_ref, k_ref, v_ref, xb_ref, bd_ref, tail_ref, raw = refs
    h = _rms(x_ref[...], g_ref[...]).astype(BF16)
    w = WIDTH_A
    if steps_per_seq is None:
        xb_ref[...] = _dot(h, w_ref[:, 3 * w:3 * w + B_CONV_DIM])
    else:
        tm = x_ref.shape[0]

        @pl.when(pl.program_id(0) % steps_per_seq == 0)
        def _():
            raw[0:HIST_ROWS, :] = jnp.zeros((HIST_ROWS, B_CONV_DIM), F32)

        raw[HIST_ROWS:HIST_ROWS + tm, :] = _dot(h, w_ref[:, 3 * w:3 * w + B_CONV_DIM])
        for r in range(0, tm, CONV_ROWS):
            xb_ref[r:r + CONV_ROWS, :] = _causal_conv_silu(
                raw[r:r + HIST_ROWS, :], raw[r + HIST_ROWS:r + HIST_ROWS + CONV_ROWS, :], cw_ref)
        tail = raw[tm:tm + HIST_ROWS, :]
        tail_ref[0] = tail
        raw[0:HIST_ROWS, :] = tail
    q_ref[...] = (_dot(h, w_ref[:, 0:w]) * Q_SCALE_A).astype(q_ref.dtype)
    k_ref[...] = _dot(h, w_ref[:, w:2 * w])
    v_ref[...] = _dot(h, w_ref[:, 2 * w:3 * w])
    bd_ref[...] = _dot(h, w_ref[:, 3 * w + B_CONV_DIM:])


def _proj(x, g, w_cat, conv_w=None, seq_len=None, tile=512):
    n, d = x.shape
    assert sum(_PROJ_WIDTHS) == w_cat.shape[1]
    tm = _token_tile(n, tile)
    tok = lambda wdt: pl.BlockSpec((tm, wdt), lambda i: (i, 0))
    out_shape = [jax.ShapeDtypeStruct((n, wdt), BF16 if i == 0 else F32) for i, wdt in enumerate(_PROJ_WIDTHS)]
    out_specs = [tok(wdt) for wdt in _PROJ_WIDTHS]
    in_specs = [tok(d), _resident((1, d)), _resident(w_cat.shape)]
    args = [x, g, w_cat]
    scratch = []
    steps_per_seq = None
    if conv_w is not None:
        assert seq_len % tm == 0 and tm % CONV_ROWS == 0
        steps_per_seq = seq_len // tm
        in_specs.append(_resident(conv_w.shape))
        args.append(conv_w)
        out_shape.append(jax.ShapeDtypeStruct((n // seq_len, HIST_ROWS, B_CONV_DIM), F32))
        out_specs.append(pl.BlockSpec((1, HIST_ROWS, B_CONV_DIM), lambda i: (i // steps_per_seq, 0, 0)))
        scratch.append(pltpu.VMEM((HIST_ROWS + tm, B_CONV_DIM), F32))
    return pl.pallas_call(
        functools.partial(_proj_body, steps_per_seq=steps_per_seq),
        out_shape=tuple(out_shape),
        grid=(n // tm,),
        in_specs=in_specs,
        out_specs=tuple(out_specs),
        scratch_shapes=scratch,
        compiler_params=_params(1),
        name="in_proj",
    )(*args)


def _pair_masks(rows):
    lane = lax.broadcasted_iota(jnp.int32, (rows, LANES), 1)
    return lane < HEAD_DIM_A


Q_TILE = 2 * CHUNK
K_TILE = A_REACH + Q_TILE


def _attn_prompt_body(q_ref, k_ref, v_ref, bias_ref, o_ref, kpad, vpad):
    s_len = q_ref.shape[1]
    n_pairs = N_HEADS_A // 2

    kpad[0:A_REACH, :] = jnp.zeros((A_REACH, WIDTH_A), BF16)
    vpad[0:A_REACH, :] = jnp.zeros((A_REACH, WIDTH_A), BF16)

    def fill(i, carry):
        r = pl.multiple_of(i * 256, 256)
        kpad[pl.ds(A_REACH + r, 256), :] = k_ref[0, pl.ds(r, 256), :].astype(BF16)
        vpad[pl.ds(A_REACH + r, 256), :] = v_ref[0, pl.ds(r, 256), :].astype(BF16)
        return carry

    lax.fori_loop(0, s_len // 256, fill, 0)

    low = _pair_masks(Q_TILE)
    kcol = lax.broadcasted_iota(jnp.int32, (1, K_TILE), 1)

    def step(it, carry):
        start = pl.multiple_of(it * Q_TILE, Q_TILE)
        exists = (kcol + start) >= A_REACH
        scores = []
        for hp in range(n_pairs):
            sl = slice(hp * LANES, (hp + 1) * LANES)
            qp = q_ref[0, pl.ds(start, Q_TILE), sl]
            zero = jnp.zeros_like(qp)
            lhs = jnp.concatenate([jnp.where(low, qp, zero), jnp.where(low, zero, qp)], axis=0)
            scores.append(_dot_nt(lhs, kpad[pl.ds(start, K_TILE), sl]))
        probs, inv = [], []
        for hp in range(n_pairs):
            s = jnp.where(exists, scores[hp] + bias_ref[hp], -jnp.inf)
            m = jnp.max(s, axis=-1, keepdims=True)
            p = jnp.exp2(s - m)
            inv.append(1.0 / jnp.sum(p, axis=-1, keepdims=True))
            probs.append(p.astype(BF16))
        for hp in range(n_pairs):
            sl = slice(hp * LANES, (hp + 1) * LANES)
            o = _dot(probs[hp], vpad[pl.ds(start, K_TILE), sl]) * inv[hp]
            o_ref[0, pl.ds(start, Q_TILE), sl] = jnp.where(low, o[:Q_TILE], o[Q_TILE:]).astype(o_ref.dtype)
        return carry

    lax.fori_loop(0, s_len // Q_TILE, step, 0)


def _attn_prompt(q, k, v, bias):
    b, s_len, w = q.shape
    assert s_len % 256 == 0
    seq = pl.BlockSpec((1, s_len, w), lambda i: (i, 0, 0))
    return pl.pallas_call(
        _attn_prompt_body,
        out_shape=jax.ShapeDtypeStruct((b, s_len, w), BF16),
        grid=(b,),
        in_specs=[seq, seq, seq, _resident(bias.shape)],
        out_specs=seq,
        scratch_shapes=[
            pltpu.VMEM((A_REACH + s_len, w), BF16),
            pltpu.VMEM((A_REACH + s_len, w), BF16),
        ],
        compiler_params=_params(1),
        name="band_attn_prompt",
    )(q, k, v, bias)


def _attn_sample_body(q_ref, k_ref, v_ref, ck_ref, cv_ref, bc_ref, bn_ref, o_ref):
    t = q_ref.shape[1]
    low = _pair_masks(t)
    for hp in range(N_HEADS_A // 2):
        sl = slice(hp * LANES, (hp + 1) * LANES)
        qp = q_ref[0, :, sl].astype(F32)
        kn = k_ref[0, :, sl].astype(BF16)
        vn = v_ref[0, :, sl].astype(BF16)
        kc = ck_ref[0, :, sl].astype(BF16)
        vc = cv_ref[0, :, sl].astype(BF16)
        outs = []
        for half in range(2):
            h = 2 * hp + half
            qm = jnp.where(low if half == 0 else ~low, qp, 0.0).astype(BF16)
            sc = _dot_nt(qm, kc) + bc_ref[h]
            sn = _dot_nt(qm, kn) + bn_ref[h]
            m = jnp.maximum(jnp.max(sc, axis=-1, keepdims=True), jnp.max(sn, axis=-1, keepdims=True))
            pc = jnp.exp2(sc - m)
            pn = jnp.exp2(sn - m)
            inv = 1.0 / (jnp.sum(pc, axis=-1, keepdims=True) + jnp.sum(pn, axis=-1, keepdims=True))
            outs.append(_dot((pc * inv).astype(BF16), vc) + _dot((pn * inv).astype(BF16), vn))
        o_ref[0, :, sl] = jnp.where(low, outs[0], outs[1]).astype(o_ref.dtype)


def _attn_sample(q, k, v, ck, cv, bias_cache, bias_new):
    b, t, w = q.shape
    p_len = ck.shape[1]
    new = pl.BlockSpec((1, t, w), lambda i: (i, 0, 0))
    old = pl.BlockSpec((1, p_len, w), lambda i: (i, 0, 0))
    return pl.pallas_call(
        _attn_sample_body,
        out_shape=jax.ShapeDtypeStruct((b, t, w), BF16),
        grid=(b,),
        in_specs=[new, new, new, old, old, _resident(bias_cache.shape), _resident(bias_new.shape)],
        out_specs=new,
        compiler_params=_params(1),
        name="band_attn_sample",
    )(q, k, v, ck, cv, bias_cache, bias_new)


DELTA_BLOCK = 512
DELTA_GROUP = 8
DELTA_SEQS = 2


def _lane_blocks(a, b):
    z = jnp.zeros_like(a)
    return jnp.concatenate([jnp.concatenate([a, z], axis=-1), jnp.concatenate([z, b], axis=-1)], axis=0)


def _delta_body(*refs, valid_len, group, conv_here):
    if conv_here:
        xb_ref, bd_ref, prev_ref, s0_ref, cw_ref, alog_ref, dtb_ref, o_ref, conv_ref, s_ref, *scratch = refs
    else:
        xb_ref, bd_ref, s0_ref, alog_ref, dtb_ref, o_ref, s_ref, *scratch = refs
    s_scr, u_scr, wq_scr, ik_scr, gl_scr = scratch
    j = pl.program_id(1)
    nb, tb = xb_ref.shape[0], xb_ref.shape[1]
    L = CHUNK
    hd = HEAD_DIM_B
    assert hd == 2 * L and N_HEADS_B == 4
    n_chunks = tb // L
    n_pairs = N_HEADS_B // 2
    assert not conv_here or n_chunks == 1

    @pl.when(j == 0)
    def _():
        for bi in range(nb):
            for p in range(n_pairs):
                s_scr[bi, p] = jnp.concatenate([s0_ref[bi, 2 * p], s0_ref[bi, 2 * p + 1]], axis=-1)

    row = lax.broadcasted_iota(jnp.int32, (L, 2 * L), 0)
    lane = lax.broadcasted_iota(jnp.int32, (L, 2 * L), 1)
    col = lane & (L - 1)
    low = lane < L
    incl = col <= row
    strict = col < row
    eye = jnp.where(col == row, 1.0, 0.0).astype(F32)
    n_levels = L.bit_length() - 1
    level = [((row >> (k + 1)) == (col >> (k + 1))) & (((row >> k) & 1) == 1) & (((col >> k) & 1) == 0)
             for k in range(n_levels)]
    r64 = lax.broadcasted_iota(jnp.int32, (L, L), 0)
    c64 = lax.broadcasted_iota(jnp.int32, (L, L), 1)
    tril = jnp.where(c64 <= r64, 1.0, 0.0).astype(BF16)
    neg_a = -jnp.exp(alog_ref[...])
    dtb = dtb_ref[...]
    tok = lax.broadcasted_iota(jnp.int32, (L, 1), 0)

    def bd_pair(y):
        z = jnp.zeros_like(y)
        return jnp.concatenate([jnp.where(low, y, z), jnp.where(low, z, y)], axis=0)

    def pair_prod(x, y):
        xh, xl = _split2(x)
        r = _dot(jnp.concatenate([xh, xl], axis=0), bd_pair(y.astype(BF16)))
        return r[0:L] + r[L:2 * L]

    def prep_chunk(bi, c):
        t0 = pl.multiple_of(c * L, L)
        xc = xb_ref[bi, pl.ds(t0, L), :]
        if conv_here:
            xc = _causal_conv_silu(prev_ref[bi], xc, cw_ref)
        bdc = bd_ref[bi, pl.ds(t0, L), :]
        beta_all = jax.nn.sigmoid(bdc)
        g_all = neg_a * _softplus(bdc + dtb)
        if valid_len is not None:
            ok = (tok + (j * tb + t0)) < valid_len
            beta_all = jnp.where(ok, beta_all, 0.0)
            g_all = jnp.where(ok, g_all, 0.0)
        g1 = g_all.astype(BF16)
        r1 = g_all - g1.astype(F32)
        g2 = r1.astype(BF16)
        g3 = (r1 - g2.astype(F32)).astype(BF16)
        gc_all = _dot(tril, g1) + _dot(tril, g2) + _dot(tril, g3)
        heads = []
        for h in range(N_HEADS_B):
            q = xc[:, h * hd:(h + 1) * hd]
            k = xc[:, WIDTH_B + h * hd:WIDTH_B + (h + 1) * hd]
            v = xc[:, 2 * WIDTH_B + h * hd:2 * WIDTH_B + (h + 1) * hd]
            q = q * lax.rsqrt(jnp.sum(q * q, axis=-1, keepdims=True) + EPS) * (hd ** -0.5)
            k = k * lax.rsqrt(jnp.sum(k * k, axis=-1, keepdims=True) + EPS)
            beta = jnp.broadcast_to(beta_all[:, h:h + 1], (L, hd))
            gc = jnp.broadcast_to(gc_all[:, N_HEADS_B + h:N_HEADS_B + h + 1], (L, hd))
            eg = jnp.exp(gc)
            g_last = gc[L - 1:L, :]
            kbeta = k * beta
            heads.append(dict(q=q, k=k, kbeta=kbeta, vbeta=v * beta, kbeg=kbeta * eg, qg=q * eg,
                              kdec=k * jnp.exp(g_last - gc), gc=gc, eg_last=jnp.exp(g_last)))
        return heads

    def phase1(bi, gi, carry):
        chunks = [gi * group + g for g in range(group)]
        heads = [prep_chunk(bi, c) for c in chunks]
        units = [(g, p) for g in range(group) for p in range(n_pairs)]
        gram = {}
        for (g, p) in units:
            a, b = heads[g][2 * p], heads[g][2 * p + 1]
            rhs = _lane_blocks(a["k"], b["k"]).astype(BF16)
            lhs = jnp.concatenate([jnp.concatenate([a["kbeta"], b["kbeta"]], axis=-1),
                                   jnp.concatenate([a["q"], b["q"]], axis=-1)], axis=0).astype(BF16)
            gram[g, p] = _dot_nt(lhs, rhs)
        a_mat, intra, t_inv = {}, {}, {}
        for (g, p) in units:
            a, b = heads[g][2 * p], heads[g][2 * p + 1]
            gc_i = jnp.where(low, a["gc"], b["gc"])
            gc_j = jnp.concatenate([a["gc"], b["gc"]], axis=0).T[0:L, :]
            decay = jnp.exp(jnp.where(incl, gc_i - gc_j, -jnp.inf))
            a_mat[g, p] = jnp.where(strict, gram[g, p][0:L] * decay, 0.0)
            intra[g, p] = gram[g, p][L:2 * L] * decay
            t_inv[g, p] = eye - jnp.where(level[0], a_mat[g, p], 0.0)
        for k in range(1, n_levels):
            y = {u: pair_prod(jnp.where(level[k], a_mat[u], 0.0), t_inv[u]) for u in units}
            t_inv = {u: t_inv[u] - pair_prod(t_inv[u], y[u]) for u in units}

        def solve_rhs(x, y):
            z = jnp.zeros((L, hd), BF16)
            return jnp.concatenate([jnp.concatenate([x[0:L], z, x[L:2 * L], z], axis=-1),
                                    jnp.concatenate([z, y[0:L], z, y[L:2 * L]], axis=-1)], axis=0)

        for (g, p) in units:
            a, b = heads[g][2 * p], heads[g][2 * p + 1]
            th, tl = _split2(t_inv[g, p])
            ah, al = _split2(jnp.concatenate([a["vbeta"], a["kbeg"]], axis=0))
            bh, bl = _split2(jnp.concatenate([b["vbeta"], b["kbeg"]], axis=0))
            rh, rl = solve_rhs(ah, bh), solve_rhs(al, bl)
            sol = _dot(th, rh) + _dot(th, rl) + _dot(tl, rh)
            c = chunks[g]
            u_scr[bi, c, p] = sol[:, 0:2 * hd]
            wq_scr[bi, c, p] = jnp.concatenate(
                [sol[:, 2 * hd:4 * hd], jnp.concatenate([a["qg"], b["qg"]], axis=-1)], axis=0).astype(BF16)
            kdec_t = jnp.concatenate([a["kdec"], b["kdec"]], axis=0).T
            ik_scr[bi, c, p] = jnp.concatenate([intra[g, p], kdec_t], axis=0).astype(BF16)
            gl_scr[bi, c, p] = jnp.broadcast_to(
                jnp.concatenate([a["eg_last"], b["eg_last"]], axis=-1), (HIST_ROWS, 2 * hd))
        return carry

    for bi in range(nb):
        lax.fori_loop(0, n_chunks // group, functools.partial(phase1, bi), 0)

    def phase2(c, carry):
        t0 = pl.multiple_of(c * L, L)
        units = [(bi, p) for bi in range(nb) for p in range(n_pairs)]
        r1, r2 = {}, {}
        for u in units:
            s = s_scr[u]
            r1[u] = _dot(wq_scr[u[0], c, u[1]], _lane_blocks(s[:, 0:hd], s[:, hd:2 * hd]).astype(BF16))
        for u in units:
            v_new = u_scr[u[0], c, u[1]] - r1[u][0:L]
            r2[u] = _dot(ik_scr[u[0], c, u[1]], _lane_blocks(v_new[:, 0:hd], v_new[:, hd:2 * hd]).astype(BF16))
        for u in units:
            bi, p = u
            o_ref[bi, pl.ds(t0, L), p * 2 * hd:(p + 1) * 2 * hd] = r1[u][L:2 * L] + r2[u][0:L]
            s_scr[u] = s_scr[u] * gl_scr[bi, c, p][0:1, :] + r2[u][L:L + hd]
        return carry

    lax.fori_loop(0, n_chunks, phase2, 0)

    @pl.when(j == pl.num_programs(1) - 1)
    def _():
        if conv_here:
            last = L if valid_len is None else valid_len
            conv_ref[...] = xb_ref[:, last - (CONV_W - 1):last, :]
        for bi in range(nb):
            for h in range(N_HEADS_B):
                s_ref[bi, h] = s_scr[bi, h // 2][:, (h % 2) * hd:(h % 2 + 1) * hd]


def _delta(xb, bd, s0, alog_row, dtb_row, prev=None, conv_w=None):
    b, t, cdim = xb.shape
    conv_here = conv_w is not None
    valid_len = None
    if t % CHUNK:
        assert CONV_W - 1 <= t < CHUNK
        valid_len = t
        xb = jnp.pad(xb, ((0, 0), (0, CHUNK - t), (0, 0)))
        bd = jnp.pad(bd, ((0, 0), (0, CHUNK - t), (0, 0)))
    tp = xb.shape[1]
    tb = _token_tile(tp, DELTA_BLOCK)
    n_chunks = tb // CHUNK
    group = _token_tile(n_chunks, DELTA_GROUP)
    n_pairs = N_HEADS_B // 2
    nb = _token_tile(b, DELTA_SEQS)
    tok = lambda wdt: pl.BlockSpec((nb, tb, wdt), lambda i, j: (i, j, 0))
    state = pl.BlockSpec((nb, N_HEADS_B, HEAD_DIM_B, HEAD_DIM_B), lambda i, j: (i, 0, 0, 0))
    conv_state = pl.BlockSpec((nb, CONV_W - 1, cdim), lambda i, j: (i, 0, 0))
    o_shape = jax.ShapeDtypeStruct((b, tp, WIDTH_B), F32)
    s_shape = jax.ShapeDtypeStruct((b, N_HEADS_B, HEAD_DIM_B, HEAD_DIM_B), F32)
    if conv_here:
        assert tp == CHUNK
        prev = jnp.pad(prev, ((0, 0), (HIST_ROWS - (CONV_W - 1), 0), (0, 0)))
        args = (xb, bd, prev, s0, conv_w, alog_row, dtb_row)
        in_specs = [tok(cdim), tok(LANES), pl.BlockSpec((nb, HIST_ROWS, cdim), lambda i, j: (i, 0, 0)), state,
                    _resident(conv_w.shape), _resident((1, LANES)), _resident((1, LANES))]
        out_shape = (o_shape, jax.ShapeDtypeStruct((b, CONV_W - 1, cdim), F32), s_shape)
        out_specs = (tok(WIDTH_B), conv_state, state)
    else:
        args = (xb, bd, s0, alog_row, dtb_row)
        in_specs = [tok(cdim), tok(LANES), state, _resident((1, LANES)), _resident((1, LANES))]
        out_shape = (o_shape, s_shape)
        out_specs = (tok(WIDTH_B), state)
    outs = pl.pallas_call(
        functools.partial(_delta_body, valid_len=valid_len, group=group, conv_here=conv_here),
        out_shape=out_shape,
        grid=(b // nb, tp // tb),
        in_specs=in_specs,
        out_specs=out_specs,
        scratch_shapes=[
            pltpu.VMEM((nb, n_pairs, HEAD_DIM_B, 2 * HEAD_DIM_B), F32),
            pltpu.VMEM((nb, n_chunks, n_pairs, CHUNK, 2 * HEAD_DIM_B), F32),
            pltpu.VMEM((nb, n_chunks, n_pairs, 2 * CHUNK, 2 * HEAD_DIM_B), BF16),
            pltpu.VMEM((nb, n_chunks, n_pairs, CHUNK + HEAD_DIM_B, 2 * CHUNK), BF16),
            pltpu.VMEM((nb, n_chunks, n_pairs, HIST_ROWS, 2 * HEAD_DIM_B), F32),
        ],
        compiler_params=_params(2),
        name="gated_delta",
    )(*args)
    if conv_here:
        o, conv, s = outs
        return o[:, :t], conv, s
    o, s = outs
    return o[:, :t], s


def _memkv_body(m_ref, g_ref, wk_ref, wv_ref, k_ref, v_ref, k16_ref, v16_ref):
    h = _rms(m_ref[...], g_ref[...]).astype(BF16)
    k = _dot(h, wk_ref[...])
    v = _dot(h, wv_ref[...])
    k_ref[...] = k
    v_ref[...] = v
    k16_ref[...] = k.astype(BF16)
    v16_ref[...] = v.astype(BF16)


def _memkv(mem, g, wk, wv, tile=512):
    n, d = mem.shape
    tm = _token_tile(n, tile)
    tok = pl.BlockSpec((tm, d), lambda i: (i, 0))
    return pl.pallas_call(
        _memkv_body,
        out_shape=(jax.ShapeDtypeStruct((n, d), F32),) * 2 + (jax.ShapeDtypeStruct((n, d), BF16),) * 2,
        grid=(n // tm,),
        in_specs=[tok, _resident((1, d)), _resident(wk.shape), _resident(wv.shape)],
        out_specs=(tok,) * 4,
        compiler_params=_params(1),
        name="mem_kv",
    )(mem, g, wk, wv)


def _post_body(x_ref, ya_ref, ob_ref, mk_ref, mv_ref, mixg_ref, on_ref, xg_ref,
               wz_ref, wgate_ref, wa_ref, wb_ref, wm_ref, wq_ref, wo_ref, o_ref):
    x = x_ref[0]
    d = x.shape[1]
    hd = HEAD_DIM_B
    h = _rms(x, mixg_ref[...]).astype(BF16)
    z = _dot(h, wz_ref[...])
    gate = jax.nn.sigmoid(_dot(h, wgate_ref[...]))
    ob = ob_ref[0]
    parts = []
    for i in range(N_HEADS_B):
        sl = slice(i * hd, (i + 1) * hd)
        parts.append(_rms(ob[:, sl], on_ref[...]) * _silu(z[:, sl]))
    obn = jnp.concatenate(parts, axis=-1).astype(BF16)
    merged = gate[:, :d] * _dot(ya_ref[0], wa_ref[...]) + gate[:, d:] * _dot(obn, wb_ref[...])
    x = x + _dot(merged.astype(BF16), wm_ref[...])

    hx = d // N_HEADS_X
    q = (_dot(_rms(x, xg_ref[...]).astype(BF16), wq_ref[...]) * (hx ** -0.5 * LOG2E)).astype(BF16)
    heads = [slice(i * hx, (i + 1) * hx) for i in range(N_HEADS_X)]
    scores = [_dot_nt(q[:, sl], mk_ref[0, :, sl]) for sl in heads]
    probs, inv = [], []
    for s in scores:
        p = jnp.exp2(s - jnp.max(s, axis=-1, keepdims=True))
        inv.append(1.0 / jnp.sum(p, axis=-1, keepdims=True))
        probs.append(p.astype(BF16))
    outs = [_dot(p, mv_ref[0, :, sl]) * r for p, r, sl in zip(probs, inv, heads)]
    o_ref[0] = x + _dot(jnp.concatenate(outs, axis=-1).astype(BF16), wo_ref[...])


def _post_mixer(x, ya, ob, mk, mv, lp, tile=512):
    b, t, d = x.shape
    n_mem = mk.shape[1]
    tq = _token_tile(t, tile)
    tok = lambda wdt: pl.BlockSpec((1, tq, wdt), lambda i, j: (i, j, 0))
    mem = pl.BlockSpec((1, n_mem, d), lambda i, j: (i, 0, 0))
    weights = [lp["mix_norm"], lp["b_out_norm"], lp["xattn_norm"], lp["w_z"], lp["w_gate"],
               lp["w_branch_a"], lp["w_branch_b"], lp["w_mix_out"], lp["xattn_wq"], lp["xattn_wo"]]
    return pl.pallas_call(
        _post_body,
        out_shape=jax.ShapeDtypeStruct((b, t, d), F32),
        grid=(b, t // tq),
        in_specs=[tok(d), tok(WIDTH_A), tok(WIDTH_B), mem, mem] + [_resident(w.shape) for w in weights],
        out_specs=tok(d),
        compiler_params=_params(2),
        name="post_mixer",
    )(x, ya, ob, mk, mv, *weights)


def _rel_bias(table, q_len, k_first, n_keys):
    d_min = -(k_first + n_keys - 1)
    d_max = q_len - 1 - k_first
    core = table[:, max(d_min, -REL_CLIP) + REL_CLIP:min(d_max, REL_CLIP) + REL_CLIP + 1]
    ext = jnp.pad(core, ((0, 0), (max(0, -REL_CLIP - d_min), max(0, d_max - REL_CLIP))), mode="edge")
    rev = ext[:, ::-1]
    rows = [rev[:, q_len - 1 - t:q_len - 1 - t + n_keys] for t in range(q_len)]
    return jnp.stack(rows, axis=1).astype(F32) * LOG2E


def _band_bias(table):
    t = jnp.arange(Q_TILE)
    kk = jnp.arange(K_TILE)
    bias = _rel_bias(table, Q_TILE, -A_REACH, K_TILE)
    q_chunk = t[:, None] // CHUNK
    k_chunk = kk[None, :] // CHUNK
    in_band = (k_chunk >= q_chunk) & (k_chunk <= q_chunk + A_REACH // CHUNK)
    bias = jnp.where(in_band[None], bias, -jnp.inf)
    return bias.reshape(N_HEADS_A // 2, 2 * Q_TILE, K_TILE)


def _trunk(x, lp, mem_k, mem_v, a_k_cache, a_v_cache, conv_prev, s0):
    b, t, d = x.shape
    n = b * t
    x = _ffn(x.reshape(n, d), lp["ffn1_norm"], lp["ffn1_wg"], lp["ffn1_wu"], lp["ffn1_wd"])
    if a_k_cache is None:
        qa, ka, va, xc, bd, tail = _proj(x, lp["mix_norm"], lp["w_cat"], conv_w=lp["b_conv_w"], seq_len=t)
        qa, ka, va = (a.reshape(b, t, WIDTH_A) for a in (qa, ka, va))
        ya = _attn_prompt(qa, ka, va, _band_bias(lp["a_rel_bias"]))
        keep = min(A_REACH, t)
        new_k, new_v = ka[:, t - keep:], va[:, t - keep:]
        ob, new_s = _delta(xc.reshape(b, t, B_CONV_DIM), bd.reshape(b, t, LANES), s0,
                           lp["alog_row"], lp["dtb_row"])
        new_conv = tail[:, HIST_ROWS - (CONV_W - 1):]
    else:
        qa, ka, va, xb, bd = _proj(x, lp["mix_norm"], lp["w_cat"])
        qa, ka, va = (a.reshape(b, t, WIDTH_A) for a in (qa, ka, va))
        p_len = a_k_cache.shape[1]
        ya = _attn_sample(
            qa, ka, va, a_k_cache.reshape(b, p_len, WIDTH_A), a_v_cache.reshape(b, p_len, WIDTH_A),
            _rel_bias(lp["a_rel_bias"], t, -p_len, p_len),
            _rel_bias(lp["a_rel_bias"], t, 0, t))
        new_k, new_v = ka, va
        ob, new_conv, new_s = _delta(xb.reshape(b, t, B_CONV_DIM), bd.reshape(b, t, LANES), s0,
                                     lp["alog_row"], lp["dtb_row"], prev=conv_prev, conv_w=lp["b_conv_w"])
    x = _post_mixer(x.reshape(b, t, d), ya, ob, mem_k, mem_v, lp)
    y = _ffn(x.reshape(n, d), lp["ffn2_norm"], lp["ffn2_wg"], lp["ffn2_wu"], lp["ffn2_wd"],
             final_g=lp["final_norm"])
    heads = (b, -1, N_HEADS_A, HEAD_DIM_A)
    return y.reshape(b, t, d), new_k.reshape(heads), new_v.reshape(heads), new_conv, new_s


def kernel(x_prompt, x_sample, cache_a_k, cache_a_v, state_b_conv, state_b_s, cache_mem_k, cache_mem_v, mem_prompt, ffn1_norm, ffn1_w_gate, ffn1_w_up, ffn1_w_down, mix_norm, w_in, a_rel_bias, b_conv_w, b_a_log, b_dt_bias, b_out_norm, w_branch_a, w_branch_b, w_mix_out, xattn_norm, mem_norm, xattn_wq, xattn_wk, xattn_wv, xattn_wo, ffn2_norm, ffn2_w_gate, ffn2_w_up, ffn2_w_down, final_norm):
    depth = ffn1_norm.shape[0]
    assert depth == 1
    l = 0
    d = x_prompt.shape[-1]
    bp = x_prompt.shape[0]
    n_mem = mem_prompt.shape[1]

    w = w_in[l]
    off_b = 3 * WIDTH_A
    off_z = off_b + B_CONV_DIM
    off_beta = off_z + WIDTH_B
    off_gate = off_beta + 2 * N_HEADS_B
    small = jnp.pad(w[:, off_beta:off_gate], ((0, 0), (0, LANES - 2 * N_HEADS_B)))
    w_cat = jnp.concatenate([w[:, :off_z], small], axis=1).astype(BF16)

    def lane_row(vec):
        return jnp.pad(vec.astype(F32), (N_HEADS_B, LANES - 2 * N_HEADS_B)).reshape(1, LANES)

    def row(vec):
        return vec.astype(F32).reshape(1, -1)

    lp = {
        "ffn1_norm": row(ffn1_norm[l]), "ffn1_wg": ffn1_w_gate[l].astype(BF16),
        "ffn1_wu": ffn1_w_up[l].astype(BF16), "ffn1_wd": ffn1_w_down[l].astype(BF16),
        "mix_norm": row(mix_norm[l]), "w_cat": w_cat, "w_z": w[:, off_z:off_beta].astype(BF16),
        "w_gate": w[:, off_gate:].astype(BF16), "a_rel_bias": a_rel_bias[l],
        "b_conv_w": b_conv_w[l], "alog_row": lane_row(b_a_log[l]), "dtb_row": lane_row(b_dt_bias[l]),
        "b_out_norm": row(b_out_norm[l]), "w_branch_a": w_branch_a[l].astype(BF16),
        "w_branch_b": w_branch_b[l].astype(BF16), "w_mix_out": w_mix_out[l].astype(BF16),
        "xattn_norm": row(xattn_norm[l]), "xattn_wq": xattn_wq[l].astype(BF16),
        "xattn_wo": xattn_wo[l].astype(BF16), "ffn2_norm": row(ffn2_norm[l]),
        "ffn2_wg": ffn2_w_gate[l].astype(BF16), "ffn2_wu": ffn2_w_up[l].astype(BF16),
        "ffn2_wd": ffn2_w_down[l].astype(BF16), "final_norm": row(final_norm),
    }

    mk_p, mv_p, mk16, mv16 = _memkv(mem_prompt.reshape(bp * n_mem, d), row(mem_norm[l]),
                                    xattn_wk[l].astype(BF16), xattn_wv[l].astype(BF16))
    s_zero = jnp.zeros((bp, N_HEADS_B, HEAD_DIM_B, HEAD_DIM_B), F32)
    yp, p_ak, p_av, p_cv, p_sb = _trunk(x_prompt, lp, mk16.reshape(bp, n_mem, d), mv16.reshape(bp, n_mem, d),
                                        None, None, None, s_zero)

    bs = x_sample.shape[0]
    ys, s_ak, s_av, s_cv, s_sb = _trunk(
        x_sample, lp, cache_mem_k[l].reshape(bs, n_mem, d).astype(BF16),
        cache_mem_v[l].reshape(bs, n_mem, d).astype(BF16),
        cache_a_k[l], cache_a_v[l], state_b_conv[l], state_b_s[l])

    mem_heads = (1, bp, n_mem, N_HEADS_X, d // N_HEADS_X)
    return (yp, ys, p_ak[None], p_av[None], p_cv[None], p_sb[None],
            mk_p.reshape(mem_heads), mv_p.reshape(mem_heads),
            s_ak[None], s_av[None], s_cv[None], s_sb[None])
```

```python
import functools

import jax
import jax.numpy as jnp
from jax import lax
from jax.experimental import pallas as pl
from jax.experimental.pallas import tpu as pltpu

F32 = jnp.float32
BF16 = jnp.bfloat16
EPS = 1e-6

CHUNK = 64
A_REACH = 8 * CHUNK
REL_CLIP = 128
N_HEADS_A = 8
HEAD_DIM_A = 64
WIDTH_A = N_HEADS_A * HEAD_DIM_A
N_HEADS_B = 4
HEAD_DIM_B = 128
WIDTH_B = N_HEADS_B * HEAD_DIM_B
CONV_W = 4
B_CONV_DIM = 3 * WIDTH_B
N_HEADS_X = 4
LANES = 128

VMEM_LIMIT_BYTES = 56 * 1024 * 1024


def _params(n_grid_axes):
    return pltpu.CompilerParams(
        dimension_semantics=("arbitrary",) * n_grid_axes,
        vmem_limit_bytes=VMEM_LIMIT_BYTES,
    )


def _resident(shape):
    nd = len(shape)
    return pl.BlockSpec(shape, lambda *_: (0,) * nd, pipeline_mode=pl.Buffered(1))


def _rms(x, g):
    return x * lax.rsqrt(jnp.mean(x * x, axis=-1, keepdims=True) + EPS) * g


def _silu(x):
    return x * jax.nn.sigmoid(x)


def _softplus(x):
    return jnp.maximum(x, 0.0) + jnp.log1p(jnp.exp(-jnp.abs(x)))


def _dot(a, b):
    return jnp.dot(a, b, preferred_element_type=F32)


def _dot_nt(a, b):
    return lax.dot_general(a, b, (((1,), (1,)), ((), ())), preferred_element_type=F32)


def _split2(x):
    hi = x.astype(BF16)
    lo = (x - hi.astype(F32)).astype(BF16)
    return hi, lo


def _token_tile(n, want):
    t = min(want, n)
    while n % t:
        t //= 2
    return t


def _ffn_body(*refs, final):
    if final:
        x_ref, g_ref, wg_ref, wu_ref, wd_ref, fg_ref, o_ref = refs
    else:
        x_ref, g_ref, wg_ref, wu_ref, wd_ref, o_ref = refs
    x = x_ref[...]
    h = _rms(x, g_ref[...]).astype(BF16)
    gate = _dot(h, wg_ref[...])
    up = _dot(h, wu_ref[...])
    a = (_silu(gate) * up).astype(BF16)
    y = x + 0.5 * _dot(a, wd_ref[...])
    if final:
        y = _rms(y, fg_ref[...])
    o_ref[...] = y


def _ffn(x, g, wg, wu, wd, final_g=None, tile=512):
    n, d = x.shape
    f = wg.shape[1]
    tm = _token_tile(n, tile)
    final = final_g is not None
    in_specs = [
        pl.BlockSpec((tm, d), lambda i: (i, 0)),
        _resident((1, d)),
        _resident((d, f)),
        _resident((d, f)),
        _resident((f, d)),
    ]
    args = [x, g, wg, wu, wd]
    if final:
        in_specs.append(_resident((1, d)))
        args.append(final_g)
    return pl.pallas_call(
        functools.partial(_ffn_body, final=final),
        out_shape=jax.ShapeDtypeStruct((n, d), F32),
        grid=(n // tm,),
        in_specs=in_specs,
        out_specs=pl.BlockSpec((tm, d), lambda i: (i, 0)),
        compiler_params=_params(1),
        name="ffn_final" if final else "ffn",
    )(*args)


LOG2E = 1.4426950408889634
Q_SCALE_A = HEAD_DIM_A ** -0.5 * LOG2E
HIST_ROWS = 8
CONV_ROWS = 64
_PROJ_WIDTHS = (WIDTH_A, WIDTH_A, WIDTH_A, B_CONV_DIM, LANES)


def _causal_conv_silu(before, cur, cw_ref):
    n = cur.shape[0]
    win = jnp.concatenate([before, cur], axis=0)
    y = cur * cw_ref[CONV_W - 1:CONV_W, :]
    for i in range(CONV_W - 1):
        lo = HIST_ROWS - (CONV_W - 1) + i
        y = y + win[lo:lo + n] * cw_ref[i:i + 1, :]
    return _silu(y)


def _proj_body(*refs, steps_per_seq):
    if steps_per_seq is None:
        x_ref, g_ref, w_ref, q_ref, k_ref, v_ref, xb_ref, bd_ref = refs
    else:
        x_ref, g_ref, w_ref, cw_ref, q_ref, k_ref, v_ref, xb_ref, bd_ref, tail_ref, kt_ref, vt_ref, raw = refs
    h = _rms(x_ref[...], g_ref[...]).astype(BF16)
    w = WIDTH_A
    if steps_per_seq is None:
        xb_ref[...] = _dot(h, w_ref[:, 3 * w:3 * w + B_CONV_DIM])
    else:
        tm = x_ref.shape[0]

        @pl.when(pl.program_id(0) % steps_per_seq == 0)
        def _():
            raw[0:HIST_ROWS, :] = jnp.zeros((HIST_ROWS, B_CONV_DIM), F32)

        raw[HIST_ROWS:HIST_ROWS + tm, :] = _dot(h, w_ref[:, 3 * w:3 * w + B_CONV_DIM])
        for r in range(0, tm, CONV_ROWS):
            xb_ref[r:r + CONV_ROWS, :] = _causal_conv_silu(
                raw[r:r + HIST_ROWS, :], raw[r + HIST_ROWS:r + HIST_ROWS + CONV_ROWS, :], cw_ref)
        tail = raw[tm:tm + HIST_ROWS, :]
        tail_ref[0] = tail
        raw[0:HIST_ROWS, :] = tail
    q_ref[...] = (_dot(h, w_ref[:, 0:w]) * Q_SCALE_A).astype(q_ref.dtype)
    k_ref[...] = _dot(h, w_ref[:, w:2 * w])
    v_ref[...] = _dot(h, w_ref[:, 2 * w:3 * w])
    bd_ref[...] = _dot(h, w_ref[:, 3 * w + B_CONV_DIM:])
    if steps_per_seq is not None:
        @pl.when(pl.program_id(0) % steps_per_seq == steps_per_seq - 1)
        def _():
            kt_ref[...] = k_ref[...]
            vt_ref[...] = v_ref[...]


def _proj(x, g, w_cat, conv_w=None, seq_len=None, tile=512):
    n, d = x.shape
    assert sum(_PROJ_WIDTHS) == w_cat.shape[1]
    tm = _token_tile(n, tile)
    tok = lambda wdt: pl.BlockSpec((tm, wdt), lambda i: (i, 0))
    out_shape = [jax.ShapeDtypeStruct((n, wdt), BF16 if i == 0 else F32) for i, wdt in enumerate(_PROJ_WIDTHS)]
    out_specs = [tok(wdt) for wdt in _PROJ_WIDTHS]
    in_specs = [tok(d), _resident((1, d)), _resident(w_cat.shape)]
    args = [x, g, w_cat]
    scratch = []
    steps_per_seq = None
    if conv_w is not None:
        assert seq_len % tm == 0 and tm % CONV_ROWS == 0
        steps_per_seq = seq_len // tm
        in_specs.append(_resident(conv_w.shape))
        args.append(conv_w)
        out_shape.append(jax.ShapeDtypeStruct((n // seq_len, HIST_ROWS, B_CONV_DIM), F32))
        out_specs.append(pl.BlockSpec((1, HIST_ROWS, B_CONV_DIM), lambda i: (i // steps_per_seq, 0, 0)))
        for _ in range(2):
            out_shape.append(jax.ShapeDtypeStruct((n // seq_len * tm, WIDTH_A), F32))
            out_specs.append(pl.BlockSpec((tm, WIDTH_A), lambda i: (i // steps_per_seq, 0)))
        scratch.append(pltpu.VMEM((HIST_ROWS + tm, B_CONV_DIM), F32))
    return pl.pallas_call(
        functools.partial(_proj_body, steps_per_seq=steps_per_seq),
        out_shape=tuple(out_shape),
        grid=(n // tm,),
        in_specs=in_specs,
        out_specs=tuple(out_specs),
        scratch_shapes=scratch,
        compiler_params=_params(1),
        name="in_proj",
    )(*args)


def _pair_masks(rows):
    lane = lax.broadcasted_iota(jnp.int32, (rows, LANES), 1)
    return lane < HEAD_DIM_A


Q_TILE = 2 * CHUNK
K_TILE = A_REACH + Q_TILE


def _attn_prompt_body(q_ref, k_ref, v_ref, bias_ref, o_ref, kpad, vpad):
    s_len = q_ref.shape[1]
    n_pairs = N_HEADS_A // 2

    kpad[0:A_REACH, :] = jnp.zeros((A_REACH, WIDTH_A), BF16)
    vpad[0:A_REACH, :] = jnp.zeros((A_REACH, WIDTH_A), BF16)

    def fill(i, carry):
        r = pl.multiple_of(i * 256, 256)
        kpad[pl.ds(A_REACH + r, 256), :] = k_ref[0, pl.ds(r, 256), :].astype(BF16)
        vpad[pl.ds(A_REACH + r, 256), :] = v_ref[0, pl.ds(r, 256), :].astype(BF16)
        return carry

    lax.fori_loop(0, s_len // 256, fill, 0)

    low = _pair_masks(Q_TILE)
    kcol = lax.broadcasted_iota(jnp.int32, (1, K_TILE), 1)

    def step(reaches_before_start, it, carry):
        start = pl.multiple_of(it * Q_TILE, Q_TILE)
        exists = (kcol + start) >= A_REACH
        scores = []
        for hp in range(n_pairs):
            sl = slice(hp * LANES, (hp + 1) * LANES)
            qp = q_ref[0, pl.ds(start, Q_TILE), sl]
            zero = jnp.zeros_like(qp)
            lhs = jnp.concatenate([jnp.where(low, qp, zero), jnp.where(low, zero, qp)], axis=0)
            scores.append(_dot_nt(lhs, kpad[pl.ds(start, K_TILE), sl]))
        probs, inv = [], []
        for hp in range(n_pairs):
            s = scores[hp] + bias_ref[hp]
            if reaches_before_start:
                s = jnp.where(exists, s, -jnp.inf)
            m = jnp.max(s, axis=-1, keepdims=True)
            p = jnp.exp2(s - m)
            inv.append(1.0 / jnp.sum(p, axis=-1, keepdims=True))
            probs.append(p.astype(BF16))
        for hp in range(n_pairs):
            sl = slice(hp * LANES, (hp + 1) * LANES)
            o = _dot(probs[hp], vpad[pl.ds(start, K_TILE), sl]) * inv[hp]
            o_ref[0, pl.ds(start, Q_TILE), sl] = jnp.where(low, o[:Q_TILE], o[Q_TILE:]).astype(o_ref.dtype)
        return carry

    n_steps = s_len // Q_TILE
    n_early = min(A_REACH // Q_TILE, n_steps)
    lax.fori_loop(0, n_early, functools.partial(step, True), 0)
    lax.fori_loop(n_early, n_steps, functools.partial(step, False), 0)


def _attn_prompt(q, k, v, bias):
    b, s_len, w = q.shape
    assert s_len % 256 == 0
    seq = pl.BlockSpec((1, s_len, w), lambda i: (i, 0, 0))
    return pl.pallas_call(
        _attn_prompt_body,
        out_shape=jax.ShapeDtypeStruct((b, s_len, w), BF16),
        grid=(b,),
        in_specs=[seq, seq, seq, _resident(bias.shape)],
        out_specs=seq,
        scratch_shapes=[
            pltpu.VMEM((A_REACH + s_len, w), BF16),
            pltpu.VMEM((A_REACH + s_len, w), BF16),
        ],
        compiler_params=_params(1),
        name="band_attn_prompt",
    )(q, k, v, bias)


def _attn_sample_body(q_ref, k_ref, v_ref, ck_ref, cv_ref, bc_ref, bn_ref, o_ref):
    t = q_ref.shape[1]
    low = _pair_masks(t)
    for hp in range(N_HEADS_A // 2):
        sl = slice(hp * LANES, (hp + 1) * LANES)
        qp = q_ref[0, :, sl].astype(F32)
        kn = k_ref[0, :, sl].astype(BF16)
        vn = v_ref[0, :, sl].astype(BF16)
        kc = ck_ref[0, :, sl].astype(BF16)
        vc = cv_ref[0, :, sl].astype(BF16)
        outs = []
        for half in range(2):
            h = 2 * hp + half
            qm = jnp.where(low if half == 0 else ~low, qp, 0.0).astype(BF16)
            sc = _dot_nt(qm, kc) + bc_ref[h]
            sn = _dot_nt(qm, kn) + bn_ref[h]
            m = jnp.maximum(jnp.max(sc, axis=-1, keepdims=True), jnp.max(sn, axis=-1, keepdims=True))
            pc = jnp.exp2(sc - m)
            pn = jnp.exp2(sn - m)
            inv = 1.0 / (jnp.sum(pc, axis=-1, keepdims=True) + jnp.sum(pn, axis=-1, keepdims=True))
            outs.append(_dot((pc * inv).astype(BF16), vc) + _dot((pn * inv).astype(BF16), vn))
        o_ref[0, :, sl] = jnp.where(low, outs[0], outs[1]).astype(o_ref.dtype)


def _attn_sample(q, k, v, ck, cv, bias_cache, bias_new):
    b, t, w = q.shape
    p_len = ck.shape[1]
    new = pl.BlockSpec((1, t, w), lambda i: (i, 0, 0))
    old = pl.BlockSpec((1, p_len, w), lambda i: (i, 0, 0))
    return pl.pallas_call(
        _attn_sample_body,
        out_shape=jax.ShapeDtypeStruct((b, t, w), BF16),
        grid=(b,),
        in_specs=[new, new, new, old, old, _resident(bias_cache.shape), _resident(bias_new.shape)],
        out_specs=new,
        compiler_params=_params(1),
        name="band_attn_sample",
    )(q, k, v, ck, cv, bias_cache, bias_new)


DELTA_BLOCK = 512
DELTA_GROUP = 8
DELTA_SEQS = 2


def _lane_blocks(a, b):
    z = jnp.zeros_like(a)
    return jnp.concatenate([jnp.concatenate([a, z], axis=-1), jnp.concatenate([z, b], axis=-1)], axis=0)


def _delta_body(*refs, valid_len, group, conv_here):
    if conv_here:
        xb_ref, bd_ref, prev_ref, s0_ref, cw_ref, alog_ref, dtb_ref, o_ref, conv_ref, s_ref, *scratch = refs
    else:
        xb_ref, bd_ref, s0_ref, alog_ref, dtb_ref, o_ref, s_ref, *scratch = refs
    s_scr, u_scr, wq_scr, ik_scr, gl_scr = scratch
    j = pl.program_id(1)
    nb, tb = xb_ref.shape[0], xb_ref.shape[1]
    L = CHUNK
    hd = HEAD_DIM_B
    assert hd == 2 * L and N_HEADS_B == 4
    n_chunks = tb // L
    n_pairs = N_HEADS_B // 2
    assert not conv_here or n_chunks == 1

    @pl.when(j == 0)
    def _():
        for bi in range(nb):
            for p in range(n_pairs):
                s_scr[bi, p] = jnp.concatenate([s0_ref[bi, 2 * p], s0_ref[bi, 2 * p + 1]], axis=-1)

    row = lax.broadcasted_iota(jnp.int32, (L, 2 * L), 0)
    lane = lax.broadcasted_iota(jnp.int32, (L, 2 * L), 1)
    col = lane & (L - 1)
    low = lane < L
    incl = col <= row
    strict = col < row
    eye = jnp.where(col == row, 1.0, 0.0).astype(F32)
    n_levels = L.bit_length() - 1
    level = [((row >> (k + 1)) == (col >> (k + 1))) & (((row >> k) & 1) == 1) & (((col >> k) & 1) == 0)
             for k in range(n_levels)]
    r64 = lax.broadcasted_iota(jnp.int32, (L, L), 0)
    c64 = lax.broadcasted_iota(jnp.int32, (L, L), 1)
    tril = jnp.where(c64 <= r64, 1.0, 0.0).astype(BF16)
    neg_a = -jnp.exp(alog_ref[...])
    dtb = dtb_ref[...]
    tok = lax.broadcasted_iota(jnp.int32, (L, 1), 0)

    def bd_pair(y):
        z = jnp.zeros_like(y)
        return jnp.concatenate([jnp.where(low, y, z), jnp.where(low, z, y)], axis=0)

    def pair_prod(x, y):
        return _dot(x.astype(BF16), bd_pair(y.astype(BF16)))

    def prep_chunk(bi, c):
        t0 = pl.multiple_of(c * L, L)
        xc = xb_ref[bi, pl.ds(t0, L), :]
        if conv_here:
            xc = _causal_conv_silu(prev_ref[bi], xc, cw_ref)
        bdc = bd_ref[bi, pl.ds(t0, L), :]
        beta_all = jax.nn.sigmoid(bdc)
        g_all = neg_a * _softplus(bdc + dtb)
        if valid_len is not None:
            ok = (tok + (j * tb + t0)) < valid_len
            beta_all = jnp.where(ok, beta_all, 0.0)
            g_all = jnp.where(ok, g_all, 0.0)
        g1 = g_all.astype(BF16)
        r1 = g_all - g1.astype(F32)
        g2 = r1.astype(BF16)
        g3 = (r1 - g2.astype(F32)).astype(BF16)
        gc_all = _dot(tril, g1) + _dot(tril, g2) + _dot(tril, g3)
        heads = []
        for h in range(N_HEADS_B):
            q = xc[:, h * hd:(h + 1) * hd]
            k = xc[:, WIDTH_B + h * hd:WIDTH_B + (h + 1) * hd]
            v = xc[:, 2 * WIDTH_B + h * hd:2 * WIDTH_B + (h + 1) * hd]
            q = q * lax.rsqrt(jnp.sum(q * q, axis=-1, keepdims=True) + EPS) * (hd ** -0.5)
            k = k * lax.rsqrt(jnp.sum(k * k, axis=-1, keepdims=True) + EPS)
            beta = jnp.broadcast_to(beta_all[:, h:h + 1], (L, hd))
            gc = jnp.broadcast_to(gc_all[:, N_HEADS_B + h:N_HEADS_B + h + 1], (L, hd))
            eg = jnp.exp(gc)
            g_last = gc[L - 1:L, :]
            kbeta = k * beta
            heads.append(dict(q=q, k=k, kbeta=kbeta, vbeta=v * beta, kbeg=kbeta * eg, qg=q * eg,
                              kdec=k * jnp.exp(g_last - gc), gc=gc, eg_last=jnp.exp(g_last)))
        return heads

    def phase1(bi, gi, carry):
        chunks = [gi * group + g for g in range(group)]
        heads = [prep_chunk(bi, c) for c in chunks]
        units = [(g, p) for g in range(group) for p in range(n_pairs)]
        gram = {}
        for (g, p) in units:
            a, b = heads[g][2 * p], heads[g][2 * p + 1]
            rhs = _lane_blocks(a["k"], b["k"]).astype(BF16)
            lhs = jnp.concatenate([jnp.concatenate([a["kbeta"], b["kbeta"]], axis=-1),
                                   jnp.concatenate([a["q"], b["q"]], axis=-1)], axis=0).astype(BF16)
            gram[g, p] = _dot_nt(lhs, rhs)
        a_mat, intra, t_inv = {}, {}, {}
        for (g, p) in units:
            a, b = heads[g][2 * p], heads[g][2 * p + 1]
            gc_i = jnp.where(low, a["gc"], b["gc"])
            gc_j = jnp.concatenate([a["gc"], b["gc"]], axis=0).T[0:L, :]
            decay = jnp.exp(jnp.where(incl, gc_i - gc_j, -jnp.inf))
            a_mat[g, p] = jnp.where(strict, gram[g, p][0:L] * decay, 0.0)
            intra[g, p] = gram[g, p][L:2 * L] * decay
            t_inv[g, p] = eye - jnp.where(level[0], a_mat[g, p], 0.0)
        for k in range(1, n_levels):
            y = {u: pair_prod(jnp.where(level[k], a_mat[u], 0.0), t_inv[u]) for u in units}
            t_inv = {u: t_inv[u] - pair_prod(t_inv[u], y[u]) for u in units}

        def solve_rhs(x, y):
            z = jnp.zeros((L, hd), BF16)
            return jnp.concatenate([jnp.concatenate([x[0:L], z, x[L:2 * L], z], axis=-1),
                                    jnp.concatenate([z, y[0:L], z, y[L:2 * L]], axis=-1)], axis=0)

        for (g, p) in units:
            a, b = heads[g][2 * p], heads[g][2 * p + 1]
            th, tl = _split2(t_inv[g, p])
            ah, al = _split2(jnp.concatenate([a["vbeta"], a["kbeg"]], axis=0))
            bh, bl = _split2(jnp.concatenate([b["vbeta"], b["kbeg"]], axis=0))
            rh, rl = solve_rhs(ah, bh), solve_rhs(al, bl)
            sol = _dot(th, rh) + _dot(th, rl) + _dot(tl, rh)
            c = chunks[g]
            u_scr[bi, c, p] = sol[:, 0:2 * hd]
            wq_scr[bi, c, p] = jnp.concatenate(
                [sol[:, 2 * hd:4 * hd], jnp.concatenate([a["qg"], b["qg"]], axis=-1)], axis=0).astype(BF16)
            kdec_t = jnp.concatenate([a["kdec"], b["kdec"]], axis=0).T
            ik_scr[bi, c, p] = jnp.concatenate([intra[g, p], kdec_t], axis=0).astype(BF16)
            gl_scr[bi, c, p] = jnp.broadcast_to(
                jnp.concatenate([a["eg_last"], b["eg_last"]], axis=-1), (HIST_ROWS, 2 * hd))
        return carry

    for bi in range(nb):
        lax.fori_loop(0, n_chunks // group, functools.partial(phase1, bi), 0)

    def phase2(c, carry):
        t0 = pl.multiple_of(c * L, L)
        units = [(bi, p) for bi in range(nb) for p in range(n_pairs)]
        r1, r2 = {}, {}
        for u in units:
            s = s_scr[u]
            r1[u] = _dot(wq_scr[u[0], c, u[1]], _lane_blocks(s[:, 0:hd], s[:, hd:2 * hd]).astype(BF16))
        for u in units:
            v_new = u_scr[u[0], c, u[1]] - r1[u][0:L]
            r2[u] = _dot(ik_scr[u[0], c, u[1]], _lane_blocks(v_new[:, 0:hd], v_new[:, hd:2 * hd]).astype(BF16))
        for u in units:
            bi, p = u
            o_ref[bi, pl.ds(t0, L), p * 2 * hd:(p + 1) * 2 * hd] = r1[u][L:2 * L] + r2[u][0:L]
            s_scr[u] = s_scr[u] * gl_scr[bi, c, p][0:1, :] + r2[u][L:L + hd]
        return carry

    lax.fori_loop(0, n_chunks, phase2, 0)

    @pl.when(j == pl.num_programs(1) - 1)
    def _():
        if conv_here:
            last = L if valid_len is None else valid_len
            conv_ref[...] = xb_ref[:, last - (CONV_W - 1):last, :]
        for bi in range(nb):
            for h in range(N_HEADS_B):
                s_ref[bi, h] = s_scr[bi, h // 2][:, (h % 2) * hd:(h % 2 + 1) * hd]


def _delta(xb, bd, s0, alog_row, dtb_row, prev=None, conv_w=None):
    b, t, cdim = xb.shape
    conv_here = conv_w is not None
    valid_len = None
    if t % CHUNK:
        assert CONV_W - 1 <= t < CHUNK
        valid_len = t
        xb = jnp.pad(xb, ((0, 0), (0, CHUNK - t), (0, 0)))
        bd = jnp.pad(bd, ((0, 0), (0, CHUNK - t), (0, 0)))
    tp = xb.shape[1]
    tb = _token_tile(tp, DELTA_BLOCK)
    n_chunks = tb // CHUNK
    group = _token_tile(n_chunks, DELTA_GROUP)
    n_pairs = N_HEADS_B // 2
    nb = _token_tile(b, DELTA_SEQS)
    tok = lambda wdt: pl.BlockSpec((nb, tb, wdt), lambda i, j: (i, j, 0))
    state = pl.BlockSpec((nb, N_HEADS_B, HEAD_DIM_B, HEAD_DIM_B), lambda i, j: (i, 0, 0, 0))
    conv_state = pl.BlockSpec((nb, CONV_W - 1, cdim), lambda i, j: (i, 0, 0))
    o_shape = jax.ShapeDtypeStruct((b, tp, WIDTH_B), F32)
    s_shape = jax.ShapeDtypeStruct((b, N_HEADS_B, HEAD_DIM_B, HEAD_DIM_B), F32)
    if conv_here:
        assert tp == CHUNK
        prev = jnp.pad(prev, ((0, 0), (HIST_ROWS - (CONV_W - 1), 0), (0, 0)))
        args = (xb, bd, prev, s0, conv_w, alog_row, dtb_row)
        in_specs = [tok(cdim), tok(LANES), pl.BlockSpec((nb, HIST_ROWS, cdim), lambda i, j: (i, 0, 0)), state,
                    _resident(conv_w.shape), _resident((1, LANES)), _resident((1, LANES))]
        out_shape = (o_shape, jax.ShapeDtypeStruct((b, CONV_W - 1, cdim), F32), s_shape)
        out_specs = (tok(WIDTH_B), conv_state, state)
    else:
        args = (xb, bd, s0, alog_row, dtb_row)
        in_specs = [tok(cdim), tok(LANES), state, _resident((1, LANES)), _resident((1, LANES))]
        out_shape = (o_shape, s_shape)
        out_specs = (tok(WIDTH_B), state)
    outs = pl.pallas_call(
        functools.partial(_delta_body, valid_len=valid_len, group=group, conv_here=conv_here),
        out_shape=out_shape,
        grid=(b // nb, tp // tb),
        in_specs=in_specs,
        out_specs=out_specs,
        scratch_shapes=[
            pltpu.VMEM((nb, n_pairs, HEAD_DIM_B, 2 * HEAD_DIM_B), F32),
            pltpu.VMEM((nb, n_chunks, n_pairs, CHUNK, 2 * HEAD_DIM_B), F32),
            pltpu.VMEM((nb, n_chunks, n_pairs, 2 * CHUNK, 2 * HEAD_DIM_B), BF16),
            pltpu.VMEM((nb, n_chunks, n_pairs, CHUNK + HEAD_DIM_B, 2 * CHUNK), BF16),
            pltpu.VMEM((nb, n_chunks, n_pairs, HIST_ROWS, 2 * HEAD_DIM_B), F32),
        ],
        compiler_params=_params(2),
        name="gated_delta",
    )(*args)
    if conv_here:
        o, conv, s = outs
        return o[:, :t], conv, s
    o, s = outs
    return o[:, :t], s


def _memkv_body(m_ref, g_ref, wk_ref, wv_ref, k_ref, v_ref, k16_ref, v16_ref):
    h = _rms(m_ref[...], g_ref[...]).astype(BF16)
    k = _dot(h, wk_ref[...])
    v = _dot(h, wv_ref[...])
    k_ref[...] = k
    v_ref[...] = v
    k16_ref[...] = k.astype(BF16)
    v16_ref[...] = v.astype(BF16)


def _memkv(mem, g, wk, wv, tile=512):
    n, d = mem.shape
    tm = _token_tile(n, tile)
    tok = pl.BlockSpec((tm, d), lambda i: (i, 0))
    return pl.pallas_call(
        _memkv_body,
        out_shape=(jax.ShapeDtypeStruct((n, d), F32),) * 2 + (jax.ShapeDtypeStruct((n, d), BF16),) * 2,
        grid=(n // tm,),
        in_specs=[tok, _resident((1, d)), _resident(wk.shape), _resident(wv.shape)],
        out_specs=(tok,) * 4,
        compiler_params=_params(1),
        name="mem_kv",
    )(mem, g, wk, wv)


def _post_body(x_ref, ya_ref, ob_ref, mk_ref, mv_ref, mixg_ref, on_ref, xg_ref,
               wz_ref, wgate_ref, wa_ref, wb_ref, wm_ref, wq_ref, wo_ref, o_ref):
    x = x_ref[0]
    d = x.shape[1]
    hd = HEAD_DIM_B
    h = _rms(x, mixg_ref[...]).astype(BF16)
    z = _dot(h, wz_ref[...])
    gate = jax.nn.sigmoid(_dot(h, wgate_ref[...]))
    ob = ob_ref[0]
    parts = []
    for i in range(N_HEADS_B):
        sl = slice(i * hd, (i + 1) * hd)
        parts.append(_rms(ob[:, sl], on_ref[...]) * _silu(z[:, sl]))
    obn = jnp.concatenate(parts, axis=-1).astype(BF16)
    merged = gate[:, :d] * _dot(ya_ref[0], wa_ref[...]) + gate[:, d:] * _dot(obn, wb_ref[...])
    x = x + _dot(merged.astype(BF16), wm_ref[...])

    hx = d // N_HEADS_X
    q = (_dot(_rms(x, xg_ref[...]).astype(BF16), wq_ref[...]) * (hx ** -0.5 * LOG2E)).astype(BF16)
    heads = [slice(i * hx, (i + 1) * hx) for i in range(N_HEADS_X)]
    scores = [_dot_nt(q[:, sl], mk_ref[0, :, sl]) for sl in heads]
    probs, inv = [], []
    for s in scores:
        p = jnp.exp2(s - jnp.max(s, axis=-1, keepdims=True))
        inv.append(1.0 / jnp.sum(p, axis=-1, keepdims=True))
        probs.append(p.astype(BF16))
    outs = [_dot(p, mv_ref[0, :, sl]) * r for p, r, sl in zip(probs, inv, heads)]
    o_ref[0] = x + _dot(jnp.concatenate(outs, axis=-1).astype(BF16), wo_ref[...])


def _post_mixer(x, ya, ob, mk, mv, lp, tile=512):
    b, t, d = x.shape
    n_mem = mk.shape[1]
    tq = _token_tile(t, tile)
    tok = lambda wdt: pl.BlockSpec((1, tq, wdt), lambda i, j: (i, j, 0))
    mem = pl.BlockSpec((1, n_mem, d), lambda i, j: (i, 0, 0))
    weights = [lp["mix_norm"], lp["b_out_norm"], lp["xattn_norm"], lp["w_z"], lp["w_gate"],
               lp["w_branch_a"], lp["w_branch_b"], lp["w_mix_out"], lp["xattn_wq"], lp["xattn_wo"]]
    return pl.pallas_call(
        _post_body,
        out_shape=jax.ShapeDtypeStruct((b, t, d), F32),
        grid=(b, t // tq),
        in_specs=[tok(d), tok(WIDTH_A), tok(WIDTH_B), mem, mem] + [_resident(w.shape) for w in weights],
        out_specs=tok(d),
        compiler_params=_params(2),
        name="post_mixer",
    )(x, ya, ob, mk, mv, *weights)


def _rel_bias(table, q_len, k_first, n_keys):
    d_min = -(k_first + n_keys - 1)
    d_max = q_len - 1 - k_first
    core = table[:, max(d_min, -REL_CLIP) + REL_CLIP:min(d_max, REL_CLIP) + REL_CLIP + 1]
    ext = jnp.pad(core, ((0, 0), (max(0, -REL_CLIP - d_min), max(0, d_max - REL_CLIP))), mode="edge")
    rev = ext[:, ::-1]
    m = rev.shape[1]
    assert m == n_keys + q_len - 1 and q_len >= 2
    flat = jnp.tile(jnp.roll(rev, -(q_len - 1), axis=1), (1, q_len))[:, :q_len * (m - 1)]
    return flat.reshape(-1, q_len, m - 1)[:, :, :n_keys].astype(F32) * LOG2E


def _band_bias(table):
    t = jnp.arange(Q_TILE)
    kk = jnp.arange(K_TILE)
    bias = _rel_bias(table, Q_TILE, -A_REACH, K_TILE)
    q_chunk = t[:, None] // CHUNK
    k_chunk = kk[None, :] // CHUNK
    in_band = (k_chunk >= q_chunk) & (k_chunk <= q_chunk + A_REACH // CHUNK)
    bias = jnp.where(in_band[None], bias, -jnp.inf)
    return bias.reshape(N_HEADS_A // 2, 2 * Q_TILE, K_TILE)


def _trunk(x, lp, mem_k, mem_v, a_k_cache, a_v_cache, conv_prev, s0):
    b, t, d = x.shape
    n = b * t
    x = _ffn(x.reshape(n, d), lp["ffn1_norm"], lp["ffn1_wg"], lp["ffn1_wu"], lp["ffn1_wd"])
    if a_k_cache is None:
        qa, ka, va, xc, bd, tail, kt, vt = _proj(x, lp["mix_norm"], lp["w_cat"], conv_w=lp["b_conv_w"], seq_len=t)
        qa, ka, va = (a.reshape(b, t, WIDTH_A) for a in (qa, ka, va))
        ya = _attn_prompt(qa, ka, va, _band_bias(lp["a_rel_bias"]))
        keep = min(A_REACH, t)
        last_tile = kt.shape[0] // b
        new_k, new_v = (a.reshape(b, last_tile, WIDTH_A)[:, last_tile - keep:] for a in (kt, vt))
        ob, new_s = _delta(xc.reshape(b, t, B_CONV_DIM), bd.reshape(b, t, LANES), s0,
                           lp["alog_row"], lp["dtb_row"])
        new_conv = tail[:, HIST_ROWS - (CONV_W - 1):]
    else:
        qa, ka, va, xb, bd = _proj(x, lp["mix_norm"], lp["w_cat"])
        qa, ka, va = (a.reshape(b, t, WIDTH_A) for a in (qa, ka, va))
        p_len = a_k_cache.shape[1]
        ya = _attn_sample(
            qa, ka, va, a_k_cache.reshape(b, p_len, WIDTH_A), a_v_cache.reshape(b, p_len, WIDTH_A),
            _rel_bias(lp["a_rel_bias"], t, -p_len, p_len),
            _rel_bias(lp["a_rel_bias"], t, 0, t))
        new_k, new_v = ka, va
        ob, new_conv, new_s = _delta(xb.reshape(b, t, B_CONV_DIM), bd.reshape(b, t, LANES), s0,
                                     lp["alog_row"], lp["dtb_row"], prev=conv_prev, conv_w=lp["b_conv_w"])
    x = _post_mixer(x.reshape(b, t, d), ya, ob, mem_k, mem_v, lp)
    y = _ffn(x.reshape(n, d), lp["ffn2_norm"], lp["ffn2_wg"], lp["ffn2_wu"], lp["ffn2_wd"],
             final_g=lp["final_norm"])
    heads = (b, -1, N_HEADS_A, HEAD_DIM_A)
    return y.reshape(b, t, d), new_k.reshape(heads), new_v.reshape(heads), new_conv, new_s


def kernel(x_prompt, x_sample, cache_a_k, cache_a_v, state_b_conv, state_b_s, cache_mem_k, cache_mem_v, mem_prompt, ffn1_norm, ffn1_w_gate, ffn1_w_up, ffn1_w_down, mix_norm, w_in, a_rel_bias, b_conv_w, b_a_log, b_dt_bias, b_out_norm, w_branch_a, w_branch_b, w_mix_out, xattn_norm, mem_norm, xattn_wq, xattn_wk, xattn_wv, xattn_wo, ffn2_norm, ffn2_w_gate, ffn2_w_up, ffn2_w_down, final_norm):
    depth = ffn1_norm.shape[0]
    assert depth == 1
    l = 0
    d = x_prompt.shape[-1]
    bp = x_prompt.shape[0]
    n_mem = mem_prompt.shape[1]

    w = w_in[l]
    off_b = 3 * WIDTH_A
    off_z = off_b + B_CONV_DIM
    off_beta = off_z + WIDTH_B
    off_gate = off_beta + 2 * N_HEADS_B
    small = jnp.pad(w[:, off_beta:off_gate], ((0, 0), (0, LANES - 2 * N_HEADS_B)))
    w_cat = jnp.concatenate([w[:, :off_z], small], axis=1).astype(BF16)

    def lane_row(vec):
        return jnp.pad(vec.astype(F32), (N_HEADS_B, LANES - 2 * N_HEADS_B)).reshape(1, LANES)

    def row(vec):
        return vec.astype(F32).reshape(1, -1)

    lp = {
        "ffn1_norm": row(ffn1_norm[l]), "ffn1_wg": ffn1_w_gate[l].astype(BF16),
        "ffn1_wu": ffn1_w_up[l].astype(BF16), "ffn1_wd": ffn1_w_down[l].astype(BF16),
        "mix_norm": row(mix_norm[l]), "w_cat": w_cat, "w_z": w[:, off_z:off_beta].astype(BF16),
        "w_gate": w[:, off_gate:].astype(BF16), "a_rel_bias": a_rel_bias[l],
        "b_conv_w": b_conv_w[l], "alog_row": lane_row(b_a_log[l]), "dtb_row": lane_row(b_dt_bias[l]),
        "b_out_norm": row(b_out_norm[l]), "w_branch_a": w_branch_a[l].astype(BF16),
        "w_branch_b": w_branch_b[l].astype(BF16), "w_mix_out": w_mix_out[l].astype(BF16),
        "xattn_norm": row(xattn_norm[l]), "xattn_wq": xattn_wq[l].astype(BF16),
        "xattn_wo": xattn_wo[l].astype(BF16), "ffn2_norm": row(ffn2_norm[l]),
        "ffn2_wg": ffn2_w_gate[l].astype(BF16), "ffn2_wu": ffn2_w_up[l].astype(BF16),
        "ffn2_wd": ffn2_w_down[l].astype(BF16), "final_norm": row(final_norm),
    }

    mk_p, mv_p, mk16, mv16 = _memkv(mem_prompt.reshape(bp * n_mem, d), row(mem_norm[l]),
                                    xattn_wk[l].astype(BF16), xattn_wv[l].astype(BF16))
    s_zero = jnp.zeros((bp, N_HEADS_B, HEAD_DIM_B, HEAD_DIM_B), F32)
    yp, p_ak, p_av, p_cv, p_sb = _trunk(x_prompt, lp, mk16.reshape(bp, n_mem, d), mv16.reshape(bp, n_mem, d),
                                        None, None, None, s_zero)

    bs = x_sample.shape[0]
    ys, s_ak, s_av, s_cv, s_sb = _trunk(
        x_sample, lp, cache_mem_k[l].reshape(bs, n_mem, d).astype(BF16),
        cache_mem_v[l].reshape(bs, n_mem, d).astype(BF16),
        cache_a_k[l], cache_a_v[l], state_b_conv[l], state_b_s[l])

    mem_heads = (1, bp, n_mem, N_HEADS_X, d // N_HEADS_X)
    return (yp, ys, p_ak[None], p_av[None], p_cv[None], p_sb[None],
            mk_p.reshape(mem_heads), mv_p.reshape(mem_heads),
            s_ak[None], s_av[None], s_cv[None], s_sb[None])
```

```python
import functools

import jax
import jax.numpy as jnp
from jax import lax
from jax.experimental import pallas as pl
from jax.experimental.pallas import tpu as pltpu

F32 = jnp.float32
BF16 = jnp.bfloat16
EPS = 1e-6

CHUNK = 64
A_REACH = 8 * CHUNK
REL_CLIP = 128
N_HEADS_A = 8
HEAD_DIM_A = 64
WIDTH_A = N_HEADS_A * HEAD_DIM_A
N_HEADS_B = 4
HEAD_DIM_B = 128
WIDTH_B = N_HEADS_B * HEAD_DIM_B
CONV_W = 4
B_CONV_DIM = 3 * WIDTH_B
N_HEADS_X = 4
LANES = 128

VMEM_LIMIT_BYTES = 56 * 1024 * 1024


def _params(n_grid_axes):
    return pltpu.CompilerParams(
        dimension_semantics=("arbitrary",) * n_grid_axes,
        vmem_limit_bytes=VMEM_LIMIT_BYTES,
    )


def _resident(shape):
    nd = len(shape)
    return pl.BlockSpec(shape, lambda *_: (0,) * nd, pipeline_mode=pl.Buffered(1))


def _rms(x, g):
    return x * lax.rsqrt(jnp.mean(x * x, axis=-1, keepdims=True) + EPS) * g


def _silu(x):
    return x * jax.nn.sigmoid(x)


def _softplus(x):
    return jnp.maximum(x, 0.0) + jnp.log1p(jnp.exp(-jnp.abs(x)))


def _dot(a, b):
    return jnp.dot(a, b, preferred_element_type=F32)


def _dot_nt(a, b):
    return lax.dot_general(a, b, (((1,), (1,)), ((), ())), preferred_element_type=F32)


def _split2(x):
    hi = x.astype(BF16)
    lo = (x - hi.astype(F32)).astype(BF16)
    return hi, lo


def _token_tile(n, want):
    t = min(want, n)
    while n % t:
        t //= 2
    return t


def _ffn_body(*refs, final):
    if final:
        x_ref, g_ref, wg_ref, wu_ref, wd_ref, fg_ref, o_ref = refs
    else:
        x_ref, g_ref, wg_ref, wu_ref, wd_ref, o_ref = refs
    x = x_ref[...]
    h = _rms(x, g_ref[...]).astype(BF16)
    gate = _dot(h, wg_ref[...])
    up = _dot(h, wu_ref[...])
    a = (_silu(gate) * up).astype(BF16)
    y = x + 0.5 * _dot(a, wd_ref[...])
    if final:
        y = _rms(y, fg_ref[...])
    o_ref[...] = y


def _ffn(x, g, wg, wu, wd, final_g=None, tile=512):
    n, d = x.shape
    f = wg.shape[1]
    tm = _token_tile(n, tile)
    final = final_g is not None
    in_specs = [
        pl.BlockSpec((tm, d), lambda i: (i, 0)),
        _resident((1, d)),
        _resident((d, f)),
        _resident((d, f)),
        _resident((f, d)),
    ]
    args = [x, g, wg, wu, wd]
    if final:
        in_specs.append(_resident((1, d)))
        args.append(final_g)
    return pl.pallas_call(
        functools.partial(_ffn_body, final=final),
        out_shape=jax.ShapeDtypeStruct((n, d), F32),
        grid=(n // tm,),
        in_specs=in_specs,
        out_specs=pl.BlockSpec((tm, d), lambda i: (i, 0)),
        compiler_params=_params(1),
        name="ffn_final" if final else "ffn",
    )(*args)


LOG2E = 1.4426950408889634
Q_SCALE_A = HEAD_DIM_A ** -0.5 * LOG2E
HIST_ROWS = 8
CONV_ROWS = 64
_PROJ_WIDTHS = (WIDTH_A, WIDTH_A, WIDTH_A, B_CONV_DIM, LANES)


def _causal_conv_silu(before, cur, cw_ref):
    n = cur.shape[0]
    win = jnp.concatenate([before, cur], axis=0)
    y = cur * cw_ref[CONV_W - 1:CONV_W, :]
    for i in range(CONV_W - 1):
        lo = HIST_ROWS - (CONV_W - 1) + i
        y = y + win[lo:lo + n] * cw_ref[i:i + 1, :]
    return _silu(y)


def _proj_body(*refs, steps_per_seq):
    if steps_per_seq is None:
        x_ref, g_ref, w_ref, q_ref, k_ref, v_ref, xb_ref, bd_ref = refs
    else:
        x_ref, g_ref, w_ref, cw_ref, q_ref, k_ref, v_ref, xb_ref, bd_ref, tail_ref, raw = refs
    h = _rms(x_ref[...], g_ref[...]).astype(BF16)
    w = WIDTH_A
    if steps_per_seq is None:
        xb_ref[...] = _dot(h, w_ref[:, 3 * w:3 * w + B_CONV_DIM])
    else:
        tm = x_ref.shape[0]

        @pl.when(pl.program_id(0) % steps_per_seq == 0)
        def _():
            raw[0:HIST_ROWS, :] = jnp.zeros((HIST_ROWS, B_CONV_DIM), F32)

        raw[HIST_ROWS:HIST_ROWS + tm, :] = _dot(h, w_ref[:, 3 * w:3 * w + B_CONV_DIM])
        for r in range(0, tm, CONV_ROWS):
            xb_ref[r:r + CONV_ROWS, :] = _causal_conv_silu(
                raw[r:r + HIST_ROWS, :], raw[r + HIST_ROWS:r + HIST_ROWS + CONV_ROWS, :], cw_ref)
        tail = raw[tm:tm + HIST_ROWS, :]
        tail_ref[0] = tail
        raw[0:HIST_ROWS, :] = tail
    q_ref[...] = (_dot(h, w_ref[:, 0:w]) * Q_SCALE_A).astype(q_ref.dtype)
    k_ref[...] = _dot(h, w_ref[:, w:2 * w])
    v_ref[...] = _dot(h, w_ref[:, 2 * w:3 * w])
    bd_ref[...] = _dot(h, w_ref[:, 3 * w + B_CONV_DIM:])


def _proj(x, g, w_cat, conv_w=None, seq_len=None, tile=512):
    n, d = x.shape
    assert sum(_PROJ_WIDTHS) == w_cat.shape[1]
    tm = _token_tile(n, tile)
    tok = lambda wdt: pl.BlockSpec((tm, wdt), lambda i: (i, 0))
    out_shape = [jax.ShapeDtypeStruct((n, wdt), BF16 if i == 0 else F32) for i, wdt in enumerate(_PROJ_WIDTHS)]
    out_specs = [tok(wdt) for wdt in _PROJ_WIDTHS]
    in_specs = [tok(d), _resident((1, d)), _resident(w_cat.shape)]
    args = [x, g, w_cat]
    scratch = []
    steps_per_seq = None
    if conv_w is not None:
        assert seq_len % tm == 0 and tm % CONV_ROWS == 0
        steps_per_seq = seq_len // tm
        in_specs.append(_resident(conv_w.shape))
        args.append(conv_w)
        out_shape.append(jax.ShapeDtypeStruct((n // seq_len, HIST_ROWS, B_CONV_DIM), F32))
        out_specs.append(pl.BlockSpec((1, HIST_ROWS, B_CONV_DIM), lambda i: (i // steps_per_seq, 0, 0)))
        scratch.append(pltpu.VMEM((HIST_ROWS + tm, B_CONV_DIM), F32))
    return pl.pallas_call(
        functools.partial(_proj_body, steps_per_seq=steps_per_seq),
        out_shape=tuple(out_shape),
        grid=(n // tm,),
        in_specs=in_specs,
        out_specs=tuple(out_specs),
        scratch_shapes=scratch,
        compiler_params=_params(1),
        name="in_proj",
    )(*args)


def _pair_masks(rows):
    lane = lax.broadcasted_iota(jnp.int32, (rows, LANES), 1)
    return lane < HEAD_DIM_A


Q_TILE = 2 * CHUNK
K_TILE = A_REACH + Q_TILE


def _attn_prompt_body(q_ref, k_ref, v_ref, bias_ref, o_ref, kpad, vpad):
    s_len = q_ref.shape[1]
    n_pairs = N_HEADS_A // 2

    kpad[0:A_REACH, :] = jnp.zeros((A_REACH, WIDTH_A), BF16)
    vpad[0:A_REACH, :] = jnp.zeros((A_REACH, WIDTH_A), BF16)

    def fill(i, carry):
        r = pl.multiple_of(i * 256, 256)
        kpad[pl.ds(A_REACH + r, 256), :] = k_ref[0, pl.ds(r, 256), :].astype(BF16)
        vpad[pl.ds(A_REACH + r, 256), :] = v_ref[0, pl.ds(r, 256), :].astype(BF16)
        return carry

    lax.fori_loop(0, s_len // 256, fill, 0)

    low = _pair_masks(Q_TILE)
    kcol = lax.broadcasted_iota(jnp.int32, (1, K_TILE), 1)

    def step(reaches_before_start, it, carry):
        start = pl.multiple_of(it * Q_TILE, Q_TILE)
        exists = (kcol + start) >= A_REACH
        scores = []
        for hp in range(n_pairs):
            sl = slice(hp * LANES, (hp + 1) * LANES)
            qp = q_ref[0, pl.ds(start, Q_TILE), sl]
            zero = jnp.zeros_like(qp)
            lhs = jnp.concatenate([jnp.where(low, qp, zero), jnp.where(low, zero, qp)], axis=0)
            scores.append(_dot_nt(lhs, kpad[pl.ds(start, K_TILE), sl]))
        probs, inv = [], []
        for hp in range(n_pairs):
            s = scores[hp] + bias_ref[hp]
            if reaches_before_start:
                s = jnp.where(exists, s, -jnp.inf)
            m = jnp.max(s, axis=-1, keepdims=True)
            p = jnp.exp2(s - m)
            inv.append(1.0 / jnp.sum(p, axis=-1, keepdims=True))
            probs.append(p.astype(BF16))
        for hp in range(n_pairs):
            sl = slice(hp * LANES, (hp + 1) * LANES)
            o = _dot(probs[hp], vpad[pl.ds(start, K_TILE), sl]) * inv[hp]
            o_ref[0, pl.ds(start, Q_TILE), sl] = jnp.where(low, o[:Q_TILE], o[Q_TILE:]).astype(o_ref.dtype)
        return carry

    n_steps = s_len // Q_TILE
    n_early = min(A_REACH // Q_TILE, n_steps)
    lax.fori_loop(0, n_early, functools.partial(step, True), 0)
    lax.fori_loop(n_early, n_steps, functools.partial(step, False), 0)


def _attn_prompt(q, k, v, bias):
    b, s_len, w = q.shape
    assert s_len % 256 == 0
    seq = pl.BlockSpec((1, s_len, w), lambda i: (i, 0, 0))
    return pl.pallas_call(
        _attn_prompt_body,
        out_shape=jax.ShapeDtypeStruct((b, s_len, w), BF16),
        grid=(b,),
        in_specs=[seq, seq, seq, _resident(bias.shape)],
        out_specs=seq,
        scratch_shapes=[
            pltpu.VMEM((A_REACH + s_len, w), BF16),
            pltpu.VMEM((A_REACH + s_len, w), BF16),
        ],
        compiler_params=_params(1),
        name="band_attn_prompt",
    )(q, k, v, bias)


ATTN_SAMPLE_SEQS = 4


def _attn_sample_body(q_ref, k_ref, v_ref, ck_ref, cv_ref, bc_ref, bn_ref, o_ref):
    nb, t = q_ref.shape[0], q_ref.shape[1]
    low = _pair_masks(t)
    units = [(bi, hp) for bi in range(nb) for hp in range(N_HEADS_A // 2)]
    sc, sn = [], []
    for bi, hp in units:
        sl = slice(hp * LANES, (hp + 1) * LANES)
        qp = q_ref[bi, :, sl].astype(F32)
        lhs = jnp.concatenate([jnp.where(low, qp, 0.0), jnp.where(low, 0.0, qp)], axis=0).astype(BF16)
        sc.append(_dot_nt(lhs, ck_ref[bi, :, sl].astype(BF16)))
        sn.append(_dot_nt(lhs, k_ref[bi, :, sl].astype(BF16)))
    pc, pn, inv = [], [], []
    for (bi, hp), s_old, s_new in zip(units, sc, sn):
        s_old = s_old + bc_ref[hp]
        s_new = s_new + bn_ref[hp]
        m = jnp.maximum(jnp.max(s_old, axis=-1, keepdims=True), jnp.max(s_new, axis=-1, keepdims=True))
        e_old = jnp.exp2(s_old - m)
        e_new = jnp.exp2(s_new - m)
        inv.append(1.0 / (jnp.sum(e_old, axis=-1, keepdims=True) + jnp.sum(e_new, axis=-1, keepdims=True)))
        pc.append(e_old.astype(BF16))
        pn.append(e_new.astype(BF16))
    for (bi, hp), p_old, p_new, r in zip(units, pc, pn, inv):
        sl = slice(hp * LANES, (hp + 1) * LANES)
        o = (_dot(p_old, cv_ref[bi, :, sl].astype(BF16)) + _dot(p_new, v_ref[bi, :, sl].astype(BF16))) * r
        o_ref[bi, :, sl] = jnp.where(low, o[:t], o[t:]).astype(o_ref.dtype)


def _attn_sample(q, k, v, ck, cv, bias_cache, bias_new):
    b, t, w = q.shape
    p_len = ck.shape[1]
    nb = _token_tile(b, ATTN_SAMPLE_SEQS)
    new = pl.BlockSpec((nb, t, w), lambda i: (i, 0, 0))
    old = pl.BlockSpec((nb, p_len, w), lambda i: (i, 0, 0))
    return pl.pallas_call(
        _attn_sample_body,
        out_shape=jax.ShapeDtypeStruct((b, t, w), BF16),
        grid=(b // nb,),
        in_specs=[new, new, new, old, old, _resident(bias_cache.shape), _resident(bias_new.shape)],
        out_specs=new,
        compiler_params=_params(1),
        name="band_attn_sample",
    )(q, k, v, ck, cv, bias_cache, bias_new)


DELTA_BLOCK = 512
DELTA_GROUP = 8
DELTA_SEQS = 2


def _lane_blocks(a, b):
    z = jnp.zeros_like(a)
    return jnp.concatenate([jnp.concatenate([a, z], axis=-1), jnp.concatenate([z, b], axis=-1)], axis=0)


def _delta_body(*refs, valid_len, group, conv_here):
    if conv_here:
        xb_ref, bd_ref, prev_ref, s0_ref, cw_ref, alog_ref, dtb_ref, o_ref, conv_ref, s_ref, *scratch = refs
    else:
        xb_ref, bd_ref, s0_ref, alog_ref, dtb_ref, o_ref, s_ref, *scratch = refs
    s_scr, u_scr, wq_scr, ik_scr, gl_scr = scratch
    j = pl.program_id(1)
    nb, tb = xb_ref.shape[0], xb_ref.shape[1]
    L = CHUNK
    hd = HEAD_DIM_B
    assert hd == 2 * L and N_HEADS_B == 4
    n_chunks = tb // L
    n_pairs = N_HEADS_B // 2
    assert not conv_here or n_chunks == 1

    @pl.when(j == 0)
    def _():
        for bi in range(nb):
            for p in range(n_pairs):
                s_scr[bi, p] = jnp.concatenate([s0_ref[bi, 2 * p], s0_ref[bi, 2 * p + 1]], axis=-1)

    row = lax.broadcasted_iota(jnp.int32, (L, 2 * L), 0)
    lane = lax.broadcasted_iota(jnp.int32, (L, 2 * L), 1)
    col = lane & (L - 1)
    low = lane < L
    incl = col <= row
    strict = col < row
    eye = jnp.where(col == row, 1.0, 0.0).astype(F32)
    n_levels = L.bit_length() - 1
    level = [((row >> (k + 1)) == (col >> (k + 1))) & (((row >> k) & 1) == 1) & (((col >> k) & 1) == 0)
             for k in range(n_levels)]
    r64 = lax.broadcasted_iota(jnp.int32, (L, L), 0)
    c64 = lax.broadcasted_iota(jnp.int32, (L, L), 1)
    tril = jnp.where(c64 <= r64, 1.0, 0.0).astype(BF16)
    neg_a = -jnp.exp(alog_ref[...])
    dtb = dtb_ref[...]
    tok = lax.broadcasted_iota(jnp.int32, (L, 1), 0)

    def bd_pair(y):
        z = jnp.zeros_like(y)
        return jnp.concatenate([jnp.where(low, y, z), jnp.where(low, z, y)], axis=0)

    def pair_prod(x, y):
        return _dot(x.astype(BF16), bd_pair(y.astype(BF16)))

    def prep_chunk(bi, c):
        t0 = c * L if isinstance(c, int) else pl.multiple_of(c * L, L)
        xc = xb_ref[bi, pl.ds(t0, L), :]
        if conv_here:
            xc = _causal_conv_silu(prev_ref[bi], xc, cw_ref)
        bdc = bd_ref[bi, pl.ds(t0, L), :]
        beta_all = jax.nn.sigmoid(bdc)
        g_all = neg_a * _softplus(bdc + dtb)
        if valid_len is not None:
            ok = (tok + (j * tb + t0)) < valid_len
            beta_all = jnp.where(ok, beta_all, 0.0)
            g_all = jnp.where(ok, g_all, 0.0)
        g1 = g_all.astype(BF16)
        r1 = g_all - g1.astype(F32)
        g2 = r1.astype(BF16)
        g3 = (r1 - g2.astype(F32)).astype(BF16)
        gc_all = _dot(tril, g1) + _dot(tril, g2) + _dot(tril, g3)
        heads = []
        for h in range(N_HEADS_B):
            q = xc[:, h * hd:(h + 1) * hd]
            k = xc[:, WIDTH_B + h * hd:WIDTH_B + (h + 1) * hd]
            v = xc[:, 2 * WIDTH_B + h * hd:2 * WIDTH_B + (h + 1) * hd]
            q = q * lax.rsqrt(jnp.sum(q * q, axis=-1, keepdims=True) + EPS) * (hd ** -0.5)
            k = k * lax.rsqrt(jnp.sum(k * k, axis=-1, keepdims=True) + EPS)
            beta = jnp.broadcast_to(beta_all[:, h:h + 1], (L, hd))
            gc = jnp.broadcast_to(gc_all[:, N_HEADS_B + h:N_HEADS_B + h + 1], (L, hd))
            eg = jnp.exp(gc)
            g_last = gc[L - 1:L, :]
            kbeta = k * beta
            heads.append(dict(q=q, k=k, kbeta=kbeta, vbeta=v * beta, kbeg=kbeta * eg, qg=q * eg,
                              kdec=k * jnp.exp(g_last - gc), gc=gc, eg_last=jnp.exp(g_last)))
        return heads

    def phase1(jobs):
        heads = [prep_chunk(bi, c) for bi, c in jobs]
        units = [(g, p) for g in range(len(jobs)) for p in range(n_pairs)]
        gram = {}
        for (g, p) in units:
            a, b = heads[g][2 * p], heads[g][2 * p + 1]
            rhs = _lane_blocks(a["k"], b["k"]).astype(BF16)
            lhs = jnp.concatenate([jnp.concatenate([a["kbeta"], b["kbeta"]], axis=-1),
                                   jnp.concatenate([a["q"], b["q"]], axis=-1)], axis=0).astype(BF16)
            gram[g, p] = _dot_nt(lhs, rhs)
        a_mat, intra, t_inv = {}, {}, {}
        for (g, p) in units:
            a, b = heads[g][2 * p], heads[g][2 * p + 1]
            gc_i = jnp.where(low, a["gc"], b["gc"])
            gc_j = jnp.concatenate([a["gc"], b["gc"]], axis=0).T[0:L, :]
            decay = jnp.exp(jnp.where(incl, gc_i - gc_j, -jnp.inf))
            a_mat[g, p] = jnp.where(strict, gram[g, p][0:L] * decay, 0.0)
            intra[g, p] = gram[g, p][L:2 * L] * decay
            t_inv[g, p] = eye - jnp.where(level[0], a_mat[g, p], 0.0)
        for k in range(1, n_levels):
            y = {u: pair_prod(jnp.where(level[k], a_mat[u], 0.0), t_inv[u]) for u in units}
            t_inv = {u: t_inv[u] - pair_prod(t_inv[u], y[u]) for u in units}

        def solve_rhs(x, y):
            z = jnp.zeros((L, hd), BF16)
            return jnp.concatenate([jnp.concatenate([x[0:L], z, x[L:2 * L], z], axis=-1),
                                    jnp.concatenate([z, y[0:L], z, y[L:2 * L]], axis=-1)], axis=0)

        for (g, p) in units:
            a, b = heads[g][2 * p], heads[g][2 * p + 1]
            th, tl = _split2(t_inv[g, p])
            ah, al = _split2(jnp.concatenate([a["vbeta"], a["kbeg"]], axis=0))
            bh, bl = _split2(jnp.concatenate([b["vbeta"], b["kbeg"]], axis=0))
            rh, rl = solve_rhs(ah, bh), solve_rhs(al, bl)
            sol = _dot(th, rh) + _dot(th, rl) + _dot(tl, rh)
            bi, c = jobs[g]
            u_scr[bi, c, p] = sol[:, 0:2 * hd]
            wq_scr[bi, c, p] = jnp.concatenate(
                [sol[:, 2 * hd:4 * hd], jnp.concatenate([a["qg"], b["qg"]], axis=-1)], axis=0).astype(BF16)
            kdec_t = jnp.concatenate([a["kdec"], b["kdec"]], axis=0).T
            ik_scr[bi, c, p] = jnp.concatenate([intra[g, p], kdec_t], axis=0).astype(BF16)
            gl_scr[bi, c, p] = jnp.broadcast_to(
                jnp.concatenate([a["eg_last"], b["eg_last"]], axis=-1), (HIST_ROWS, 2 * hd))

    if n_chunks == 1:
        phase1([(bi, 0) for bi in range(nb)])
    else:
        def group_of(bi, gi, carry):
            phase1([(bi, gi * group + g) for g in range(group)])
            return carry

        for bi in range(nb):
            lax.fori_loop(0, n_chunks // group, functools.partial(group_of, bi), 0)

    def phase2(c, carry):
        t0 = pl.multiple_of(c * L, L)
        units = [(bi, p) for bi in range(nb) for p in range(n_pairs)]
        r1, r2 = {}, {}
        for u in units:
            s = s_scr[u]
            r1[u] = _dot(wq_scr[u[0], c, u[1]], _lane_blocks(s[:, 0:hd], s[:, hd:2 * hd]).astype(BF16))
        for u in units:
            v_new = u_scr[u[0], c, u[1]] - r1[u][0:L]
            r2[u] = _dot(ik_scr[u[0], c, u[1]], _lane_blocks(v_new[:, 0:hd], v_new[:, hd:2 * hd]).astype(BF16))
        for u in units:
            bi, p = u
            o_ref[bi, pl.ds(t0, L), p * 2 * hd:(p + 1) * 2 * hd] = r1[u][L:2 * L] + r2[u][0:L]
            s_scr[u] = s_scr[u] * gl_scr[bi, c, p][0:1, :] + r2[u][L:L + hd]
        return carry

    lax.fori_loop(0, n_chunks, phase2, 0)

    @pl.when(j == pl.num_programs(1) - 1)
    def _():
        if conv_here:
            last = L if valid_len is None else valid_len
            conv_ref[...] = xb_ref[:, last - (CONV_W - 1):last, :]
        for bi in range(nb):
            for h in range(N_HEADS_B):
                s_ref[bi, h] = s_scr[bi, h // 2][:, (h % 2) * hd:(h % 2 + 1) * hd]


def _delta(xb, bd, s0, alog_row, dtb_row, prev=None, conv_w=None):
    b, t, cdim = xb.shape
    conv_here = conv_w is not None
    valid_len = None
    if t % CHUNK:
        assert CONV_W - 1 <= t < CHUNK
        valid_len = t
        xb = jnp.pad(xb, ((0, 0), (0, CHUNK - t), (0, 0)))
        bd = jnp.pad(bd, ((0, 0), (0, CHUNK - t), (0, 0)))
    tp = xb.shape[1]
    tb = _token_tile(tp, DELTA_BLOCK)
    n_chunks = tb // CHUNK
    group = _token_tile(n_chunks, DELTA_GROUP)
    n_pairs = N_HEADS_B // 2
    nb = _token_tile(b, DELTA_SEQS if n_chunks > 1 else DELTA_GROUP)
    tok = lambda wdt: pl.BlockSpec((nb, tb, wdt), lambda i, j: (i, j, 0))
    state = pl.BlockSpec((nb, N_HEADS_B, HEAD_DIM_B, HEAD_DIM_B), lambda i, j: (i, 0, 0, 0))
    conv_state = pl.BlockSpec((nb, CONV_W - 1, cdim), lambda i, j: (i, 0, 0))
    o_shape = jax.ShapeDtypeStruct((b, tp, WIDTH_B), F32)
    s_shape = jax.ShapeDtypeStruct((b, N_HEADS_B, HEAD_DIM_B, HEAD_DIM_B), F32)
    if conv_here:
        assert tp == CHUNK
        prev = jnp.pad(prev, ((0, 0), (HIST_ROWS - (CONV_W - 1), 0), (0, 0)))
        args = (xb, bd, prev, s0, conv_w, alog_row, dtb_row)
        in_specs = [tok(cdim), tok(LANES), pl.BlockSpec((nb, HIST_ROWS, cdim), lambda i, j: (i, 0, 0)), state,
                    _resident(conv_w.shape), _resident((1, LANES)), _resident((1, LANES))]
        out_shape = (o_shape, jax.ShapeDtypeStruct((b, CONV_W - 1, cdim), F32), s_shape)
        out_specs = (tok(WIDTH_B), conv_state, state)
    else:
        args = (xb, bd, s0, alog_row, dtb_row)
        in_specs = [tok(cdim), tok(LANES), state, _resident((1, LANES)), _resident((1, LANES))]
        out_shape = (o_shape, s_shape)
        out_specs = (tok(WIDTH_B), state)
    outs = pl.pallas_call(
        functools.partial(_delta_body, valid_len=valid_len, group=group, conv_here=conv_here),
        out_shape=out_shape,
        grid=(b // nb, tp // tb),
        in_specs=in_specs,
        out_specs=out_specs,
        scratch_shapes=[
            pltpu.VMEM((nb, n_pairs, HEAD_DIM_B, 2 * HEAD_DIM_B), F32),
            pltpu.VMEM((nb, n_chunks, n_pairs, CHUNK, 2 * HEAD_DIM_B), F32),
            pltpu.VMEM((nb, n_chunks, n_pairs, 2 * CHUNK, 2 * HEAD_DIM_B), BF16),
            pltpu.VMEM((nb, n_chunks, n_pairs, CHUNK + HEAD_DIM_B, 2 * CHUNK), BF16),
            pltpu.VMEM((nb, n_chunks, n_pairs, HIST_ROWS, 2 * HEAD_DIM_B), F32),
        ],
        compiler_params=_params(2),
        name="gated_delta",
    )(*args)
    if conv_here:
        o, conv, s = outs
        return o[:, :t], conv, s
    o, s = outs
    return o[:, :t], s


def _memkv_body(m_ref, g_ref, wk_ref, wv_ref, k_ref, v_ref, k16_ref, v16_ref):
    h = _rms(m_ref[...], g_ref[...]).astype(BF16)
    k = _dot(h, wk_ref[...])
    v = _dot(h, wv_ref[...])
    k_ref[...] = k
    v_ref[...] = v
    k16_ref[...] = k.astype(BF16)
    v16_ref[...] = v.astype(BF16)


def _memkv(mem, g, wk, wv, tile=512):
    n, d = mem.shape
    tm = _token_tile(n, tile)
    tok = pl.BlockSpec((tm, d), lambda i: (i, 0))
    return pl.pallas_call(
        _memkv_body,
        out_shape=(jax.ShapeDtypeStruct((n, d), F32),) * 2 + (jax.ShapeDtypeStruct((n, d), BF16),) * 2,
        grid=(n // tm,),
        in_specs=[tok, _resident((1, d)), _resident(wk.shape), _resident(wv.shape)],
        out_specs=(tok,) * 4,
        compiler_params=_params(1),
        name="mem_kv",
    )(mem, g, wk, wv)


POST_ROWS = 128


def _post_body(x_ref, ya_ref, ob_ref, mk_ref, mv_ref, mixg_ref, on_ref, xg_ref,
               wz_ref, wgate_ref, wa_ref, wb_ref, wm_ref, wq_ref, wo_ref, o_ref):
    nb, tq, d = x_ref.shape
    x = x_ref[...].reshape(nb * tq, d)
    hd = HEAD_DIM_B
    h = _rms(x, mixg_ref[...]).astype(BF16)
    z = _dot(h, wz_ref[...])
    gate = jax.nn.sigmoid(_dot(h, wgate_ref[...]))
    ob = ob_ref[...].reshape(nb * tq, WIDTH_B)
    parts = []
    for i in range(N_HEADS_B):
        sl = slice(i * hd, (i + 1) * hd)
        parts.append(_rms(ob[:, sl], on_ref[...]) * _silu(z[:, sl]))
    obn = jnp.concatenate(parts, axis=-1).astype(BF16)
    ya = ya_ref[...].reshape(nb * tq, WIDTH_A)
    merged = gate[:, :d] * _dot(ya, wa_ref[...]) + gate[:, d:] * _dot(obn, wb_ref[...])
    x = x + _dot(merged.astype(BF16), wm_ref[...])

    hx = d // N_HEADS_X
    q = (_dot(_rms(x, xg_ref[...]).astype(BF16), wq_ref[...]) * (hx ** -0.5 * LOG2E)).astype(BF16)
    units = [(bi, slice(i * hx, (i + 1) * hx)) for bi in range(nb) for i in range(N_HEADS_X)]
    scores = [_dot_nt(q[bi * tq:(bi + 1) * tq, sl], mk_ref[bi, :, sl]) for bi, sl in units]
    probs, inv = [], []
    for s in scores:
        p = jnp.exp2(s - jnp.max(s, axis=-1, keepdims=True))
        inv.append(1.0 / jnp.sum(p, axis=-1, keepdims=True))
        probs.append(p.astype(BF16))
    outs = [_dot(p, mv_ref[bi, :, sl]) * r for p, r, (bi, sl) in zip(probs, inv, units)]
    rows = [jnp.concatenate(outs[bi * N_HEADS_X:(bi + 1) * N_HEADS_X], axis=-1) for bi in range(nb)]
    o = jnp.concatenate(rows, axis=0).astype(BF16)
    o_ref[...] = (x + _dot(o, wo_ref[...])).reshape(nb, tq, d)


def _post_mixer(x, ya, ob, mk, mv, lp, tile=512):
    b, t, d = x.shape
    n_mem = mk.shape[1]
    tq = _token_tile(t, tile)
    nb = _token_tile(b, max(1, POST_ROWS // tq))
    tok = lambda wdt: pl.BlockSpec((nb, tq, wdt), lambda i, j: (i, j, 0))
    mem = pl.BlockSpec((nb, n_mem, d), lambda i, j: (i, 0, 0))
    weights = [lp["mix_norm"], lp["b_out_norm"], lp["xattn_norm"], lp["w_z"], lp["w_gate"],
               lp["w_branch_a"], lp["w_branch_b"], lp["w_mix_out"], lp["xattn_wq"], lp["xattn_wo"]]
    return pl.pallas_call(
        _post_body,
        out_shape=jax.ShapeDtypeStruct((b, t, d), F32),
        grid=(b // nb, t // tq),
        in_specs=[tok(d), tok(WIDTH_A), tok(WIDTH_B), mem, mem] + [_resident(w.shape) for w in weights],
        out_specs=tok(d),
        compiler_params=_params(2),
        name="post_mixer",
    )(x, ya, ob, mk, mv, *weights)


def _rel_bias(table, q_len, k_first, n_keys):
    d_min = -(k_first + n_keys - 1)
    d_max = q_len - 1 - k_first
    core = table[:, max(d_min, -REL_CLIP) + REL_CLIP:min(d_max, REL_CLIP) + REL_CLIP + 1]
    ext = jnp.pad(core, ((0, 0), (max(0, -REL_CLIP - d_min), max(0, d_max - REL_CLIP))), mode="edge")
    rev = ext[:, ::-1]
    m = rev.shape[1]
    assert m == n_keys + q_len - 1 and q_len >= 2
    flat = jnp.tile(jnp.roll(rev, -(q_len - 1), axis=1), (1, q_len))[:, :q_len * (m - 1)]
    return flat.reshape(-1, q_len, m - 1)[:, :, :n_keys].astype(F32) * LOG2E


def _band_bias(table):
    t = jnp.arange(Q_TILE)
    kk = jnp.arange(K_TILE)
    bias = _rel_bias(table, Q_TILE, -A_REACH, K_TILE)
    q_chunk = t[:, None] // CHUNK
    k_chunk = kk[None, :] // CHUNK
    in_band = (k_chunk >= q_chunk) & (k_chunk <= q_chunk + A_REACH // CHUNK)
    bias = jnp.where(in_band[None], bias, -jnp.inf)
    return bias.reshape(N_HEADS_A // 2, 2 * Q_TILE, K_TILE)


def _trunk(x, lp, mem_k, mem_v, a_k_cache, a_v_cache, conv_prev, s0):
    b, t, d = x.shape
    n = b * t
    x = _ffn(x.reshape(n, d), lp["ffn1_norm"], lp["ffn1_wg"], lp["ffn1_wu"], lp["ffn1_wd"])
    if a_k_cache is None:
        qa, ka, va, xc, bd, tail = _proj(x, lp["mix_norm"], lp["w_cat"], conv_w=lp["b_conv_w"], seq_len=t)
        qa, ka, va = (a.reshape(b, t, WIDTH_A) for a in (qa, ka, va))
        ya = _attn_prompt(qa, ka, va, _band_bias(lp["a_rel_bias"]))
        keep = min(A_REACH, t)
        new_k, new_v = ka[:, t - keep:], va[:, t - keep:]
        ob, new_s = _delta(xc.reshape(b, t, B_CONV_DIM), bd.reshape(b, t, LANES), s0,
                           lp["alog_row"], lp["dtb_row"])
        new_conv = tail[:, HIST_ROWS - (CONV_W - 1):]
    else:
        qa, ka, va, xb, bd = _proj(x, lp["mix_norm"], lp["w_cat"])
        qa, ka, va = (a.reshape(b, t, WIDTH_A) for a in (qa, ka, va))
        p_len = a_k_cache.shape[1]
        ya = _attn_sample(
            qa, ka, va, a_k_cache.reshape(b, p_len, WIDTH_A), a_v_cache.reshape(b, p_len, WIDTH_A),
            _rel_bias(lp["a_rel_bias"], t, -p_len, p_len).reshape(N_HEADS_A // 2, 2 * t, p_len),
            _rel_bias(lp["a_rel_bias"], t, 0, t).reshape(N_HEADS_A // 2, 2 * t, t))
        new_k, new_v = ka, va
        ob, new_conv, new_s = _delta(xb.reshape(b, t, B_CONV_DIM), bd.reshape(b, t, LANES), s0,
                                     lp["alog_row"], lp["dtb_row"], prev=conv_prev, conv_w=lp["b_conv_w"])
    x = _post_mixer(x.reshape(b, t, d), ya, ob, mem_k, mem_v, lp)
    y = _ffn(x.reshape(n, d), lp["ffn2_norm"], lp["ffn2_wg"], lp["ffn2_wu"], lp["ffn2_wd"],
             final_g=lp["final_norm"])
    heads = (b, -1, N_HEADS_A, HEAD_DIM_A)
    return y.reshape(b, t, d), new_k.reshape(heads), new_v.reshape(heads), new_conv, new_s


def kernel(x_prompt, x_sample, cache_a_k, cache_a_v, state_b_conv, state_b_s, cache_mem_k, cache_mem_v, mem_prompt, ffn1_norm, ffn1_w_gate, ffn1_w_up, ffn1_w_down, mix_norm, w_in, a_rel_bias, b_conv_w, b_a_log, b_dt_bias, b_out_norm, w_branch_a, w_branch_b, w_mix_out, xattn_norm, mem_norm, xattn_wq, xattn_wk, xattn_wv, xattn_wo, ffn2_norm, ffn2_w_gate, ffn2_w_up, ffn2_w_down, final_norm):
    depth = ffn1_norm.shape[0]
    assert depth == 1
    l = 0
    d = x_prompt.shape[-1]
    bp = x_prompt.shape[0]
    n_mem = mem_prompt.shape[1]

    w = w_in[l]
    off_b = 3 * WIDTH_A
    off_z = off_b + B_CONV_DIM
    off_beta = off_z + WIDTH_B
    off_gate = off_beta + 2 * N_HEADS_B
    small = jnp.pad(w[:, off_beta:off_gate], ((0, 0), (0, LANES - 2 * N_HEADS_B)))
    w_cat = jnp.concatenate([w[:, :off_z], small], axis=1).astype(BF16)

    def lane_row(vec):
        return jnp.pad(vec.astype(F32), (N_HEADS_B, LANES - 2 * N_HEADS_B)).reshape(1, LANES)

    def row(vec):
        return vec.astype(F32).reshape(1, -1)

    lp = {
        "ffn1_norm": row(ffn1_norm[l]), "ffn1_wg": ffn1_w_gate[l].astype(BF16),
        "ffn1_wu": ffn1_w_up[l].astype(BF16), "ffn1_wd": ffn1_w_down[l].astype(BF16),
        "mix_norm": row(mix_norm[l]), "w_cat": w_cat, "w_z": w[:, off_z:off_beta].astype(BF16),
        "w_gate": w[:, off_gate:].astype(BF16), "a_rel_bias": a_rel_bias[l],
        "b_conv_w": b_conv_w[l], "alog_row": lane_row(b_a_log[l]), "dtb_row": lane_row(b_dt_bias[l]),
        "b_out_norm": row(b_out_norm[l]), "w_branch_a": w_branch_a[l].astype(BF16),
        "w_branch_b": w_branch_b[l].astype(BF16), "w_mix_out": w_mix_out[l].astype(BF16),
        "xattn_norm": row(xattn_norm[l]), "xattn_wq": xattn_wq[l].astype(BF16),
        "xattn_wo": xattn_wo[l].astype(BF16), "ffn2_norm": row(ffn2_norm[l]),
        "ffn2_wg": ffn2_w_gate[l].astype(BF16), "ffn2_wu": ffn2_w_up[l].astype(BF16),
        "ffn2_wd": ffn2_w_down[l].astype(BF16), "final_norm": row(final_norm),
    }

    mk_p, mv_p, mk16, mv16 = _memkv(mem_prompt.reshape(bp * n_mem, d), row(mem_norm[l]),
                                    xattn_wk[l].astype(BF16), xattn_wv[l].astype(BF16))
    s_zero = jnp.zeros((bp, N_HEADS_B, HEAD_DIM_B, HEAD_DIM_B), F32)
    yp, p_ak, p_av, p_cv, p_sb = _trunk(x_prompt, lp, mk16.reshape(bp, n_mem, d), mv16.reshape(bp, n_mem, d),
                                        None, None, None, s_zero)

    bs = x_sample.shape[0]
    ys, s_ak, s_av, s_cv, s_sb = _trunk(
        x_sample, lp, cache_mem_k[l].reshape(bs, n_mem, d).astype(BF16),
        cache_mem_v[l].reshape(bs, n_mem, d).astype(BF16),
        cache_a_k[l], cache_a_v[l], state_b_conv[l], state_b_s[l])

    mem_heads = (1, bp, n_mem, N_HEADS_X, d // N_HEADS_X)
    return (yp, ys, p_ak[None], p_av[None], p_cv[None], p_sb[None],
            mk_p.reshape(mem_heads), mv_p.reshape(mem_heads),
            s_ak[None], s_av[None], s_cv[None], s_sb[None])
```

```python
import functools

import jax
import jax.numpy as jnp
from jax import lax
from jax.experimental import pallas as pl
from jax.experimental.pallas import tpu as pltpu

F32 = jnp.float32
BF16 = jnp.bfloat16
EPS = 1e-6

CHUNK = 64
A_REACH = 8 * CHUNK
REL_CLIP = 128
N_HEADS_A = 8
HEAD_DIM_A = 64
WIDTH_A = N_HEADS_A * HEAD_DIM_A
N_HEADS_B = 4
HEAD_DIM_B = 128
WIDTH_B = N_HEADS_B * HEAD_DIM_B
CONV_W = 4
B_CONV_DIM = 3 * WIDTH_B
N_HEADS_X = 4
LANES = 128

VMEM_LIMIT_BYTES = 56 * 1024 * 1024


def _params(n_grid_axes):
    return pltpu.CompilerParams(
        dimension_semantics=("arbitrary",) * n_grid_axes,
        vmem_limit_bytes=VMEM_LIMIT_BYTES,
    )


def _resident(shape):
    nd = len(shape)
    return pl.BlockSpec(shape, lambda *_: (0,) * nd, pipeline_mode=pl.Buffered(1))


def _rms(x, g):
    return x * lax.rsqrt(jnp.mean(x * x, axis=-1, keepdims=True) + EPS) * g


def _silu(x):
    return x * jax.nn.sigmoid(x)


def _softplus(x):
    return jnp.maximum(x, 0.0) + jnp.log1p(jnp.exp(-jnp.abs(x)))


def _dot(a, b):
    return jnp.dot(a, b, preferred_element_type=F32)


def _dot_nt(a, b):
    return lax.dot_general(a, b, (((1,), (1,)), ((), ())), preferred_element_type=F32)


def _split2(x):
    hi = x.astype(BF16)
    lo = (x - hi.astype(F32)).astype(BF16)
    return hi, lo


def _token_tile(n, want):
    t = min(want, n)
    while n % t:
        t //= 2
    return t


def _ffn_body(*refs, final):
    if final:
        x_ref, g_ref, wg_ref, wu_ref, wd_ref, fg_ref, o_ref = refs
    else:
        x_ref, g_ref, wg_ref, wu_ref, wd_ref, o_ref = refs
    x = x_ref[...]
    h = _rms(x, g_ref[...]).astype(BF16)
    gate = _dot(h, wg_ref[...])
    up = _dot(h, wu_ref[...])
    a = (_silu(gate) * up).astype(BF16)
    y = x + 0.5 * _dot(a, wd_ref[...])
    if final:
        y = _rms(y, fg_ref[...])
    o_ref[...] = y


def _ffn(x, g, wg, wu, wd, final_g=None, tile=512):
    n, d = x.shape
    f = wg.shape[1]
    tm = _token_tile(n, tile)
    final = final_g is not None
    in_specs = [
        pl.BlockSpec((tm, d), lambda i: (i, 0)),
        _resident((1, d)),
        _resident((d, f)),
        _resident((d, f)),
        _resident((f, d)),
    ]
    args = [x, g, wg, wu, wd]
    if final:
        in_specs.append(_resident((1, d)))
        args.append(final_g)
    return pl.pallas_call(
        functools.partial(_ffn_body, final=final),
        out_shape=jax.ShapeDtypeStruct((n, d), F32),
        grid=(n // tm,),
        in_specs=in_specs,
        out_specs=pl.BlockSpec((tm, d), lambda i: (i, 0)),
        compiler_params=_params(1),
        name="ffn_final" if final else "ffn",
    )(*args)


LOG2E = 1.4426950408889634
Q_SCALE_A = HEAD_DIM_A ** -0.5 * LOG2E
HIST_ROWS = 8
CONV_ROWS = 64
_PROJ_WIDTHS = (WIDTH_A, WIDTH_A, WIDTH_A, B_CONV_DIM, LANES)


def _causal_conv_silu(before, cur, cw_ref):
    n = cur.shape[0]
    win = jnp.concatenate([before, cur], axis=0)
    y = cur * cw_ref[CONV_W - 1:CONV_W, :]
    for i in range(CONV_W - 1):
        lo = HIST_ROWS - (CONV_W - 1) + i
        y = y + win[lo:lo + n] * cw_ref[i:i + 1, :]
    return _silu(y)


def _proj_body(*refs, steps_per_seq):
    if steps_per_seq is None:
        x_ref, g_ref, w_ref, q_ref, k_ref, v_ref, xb_ref, bd_ref = refs
    else:
        x_ref, g_ref, w_ref, cw_ref, q_ref, k_ref, v_ref, xb_ref, bd_ref, tail_ref, raw = refs
    h = _rms(x_ref[...], g_ref[...]).astype(BF16)
    w = WIDTH_A
    if steps_per_seq is None:
        xb_ref[...] = _dot(h, w_ref[:, 3 * w:3 * w + B_CONV_DIM])
    else:
        tm = x_ref.shape[0]

        @pl.when(pl.program_id(0) % steps_per_seq == 0)
        def _():
            raw[0:HIST_ROWS, :] = jnp.zeros((HIST_ROWS, B_CONV_DIM), F32)

        raw[HIST_ROWS:HIST_ROWS + tm, :] = _dot(h, w_ref[:, 3 * w:3 * w + B_CONV_DIM])
        for r in range(0, tm, CONV_ROWS):
            xb_ref[r:r + CONV_ROWS, :] = _causal_conv_silu(
                raw[r:r + HIST_ROWS, :], raw[r + HIST_ROWS:r + HIST_ROWS + CONV_ROWS, :], cw_ref)
        tail = raw[tm:tm + HIST_ROWS, :]
        tail_ref[0] = tail
        raw[0:HIST_ROWS, :] = tail
    q_ref[...] = (_dot(h, w_ref[:, 0:w]) * Q_SCALE_A).astype(q_ref.dtype)
    k_ref[...] = _dot(h, w_ref[:, w:2 * w])
    v_ref[...] = _dot(h, w_ref[:, 2 * w:3 * w])
    bd_ref[...] = _dot(h, w_ref[:, 3 * w + B_CONV_DIM:])


def _proj(x, g, w_cat, conv_w=None, seq_len=None, tile=512):
    n, d = x.shape
    assert sum(_PROJ_WIDTHS) == w_cat.shape[1]
    tm = _token_tile(n, tile)
    tok = lambda wdt: pl.BlockSpec((tm, wdt), lambda i: (i, 0))
    out_shape = [jax.ShapeDtypeStruct((n, wdt), BF16 if i == 0 else F32) for i, wdt in enumerate(_PROJ_WIDTHS)]
    out_specs = [tok(wdt) for wdt in _PROJ_WIDTHS]
    in_specs = [tok(d), _resident((1, d)), _resident(w_cat.shape)]
    args = [x, g, w_cat]
    scratch = []
    steps_per_seq = None
    if conv_w is not None:
        assert seq_len % tm == 0 and tm % CONV_ROWS == 0
        steps_per_seq = seq_len // tm
        in_specs.append(_resident(conv_w.shape))
        args.append(conv_w)
        out_shape.append(jax.ShapeDtypeStruct((n // seq_len, HIST_ROWS, B_CONV_DIM), F32))
        out_specs.append(pl.BlockSpec((1, HIST_ROWS, B_CONV_DIM), lambda i: (i // steps_per_seq, 0, 0)))
        scratch.append(pltpu.VMEM((HIST_ROWS + tm, B_CONV_DIM), F32))
    return pl.pallas_call(
        functools.partial(_proj_body, steps_per_seq=steps_per_seq),
        out_shape=tuple(out_shape),
        grid=(n // tm,),
        in_specs=in_specs,
        out_specs=tuple(out_specs),
        scratch_shapes=scratch,
        compiler_params=_params(1),
        name="in_proj",
    )(*args)


def _pair_masks(rows):
    lane = lax.broadcasted_iota(jnp.int32, (rows, LANES), 1)
    return lane < HEAD_DIM_A


Q_TILE = 2 * CHUNK
K_TILE = A_REACH + Q_TILE


def _attn_prompt_body(q_ref, k_ref, v_ref, bias_ref, o_ref, kpad, vpad):
    s_len = q_ref.shape[1]
    n_pairs = N_HEADS_A // 2

    kpad[0:A_REACH, :] = jnp.zeros((A_REACH, WIDTH_A), BF16)
    vpad[0:A_REACH, :] = jnp.zeros((A_REACH, WIDTH_A), BF16)

    def fill(i, carry):
        r = pl.multiple_of(i * 256, 256)
        kpad[pl.ds(A_REACH + r, 256), :] = k_ref[0, pl.ds(r, 256), :].astype(BF16)
        vpad[pl.ds(A_REACH + r, 256), :] = v_ref[0, pl.ds(r, 256), :].astype(BF16)
        return carry

    lax.fori_loop(0, s_len // 256, fill, 0)

    low = _pair_masks(Q_TILE)
    kcol = lax.broadcasted_iota(jnp.int32, (1, K_TILE), 1)

    def step(reaches_before_start, it, carry):
        start = pl.multiple_of(it * Q_TILE, Q_TILE)
        exists = (kcol + start) >= A_REACH
        scores = []
        for hp in range(n_pairs):
            sl = slice(hp * LANES, (hp + 1) * LANES)
            qp = q_ref[0, pl.ds(start, Q_TILE), sl]
            zero = jnp.zeros_like(qp)
            lhs = jnp.concatenate([jnp.where(low, qp, zero), jnp.where(low, zero, qp)], axis=0)
            scores.append(_dot_nt(lhs, kpad[pl.ds(start, K_TILE), sl]))
        probs, inv = [], []
        for hp in range(n_pairs):
            s = scores[hp] + bias_ref[hp]
            if reaches_before_start:
                s = jnp.where(exists, s, -jnp.inf)
            m = jnp.max(s, axis=-1, keepdims=True)
            p = jnp.exp2(s - m)
            inv.append(1.0 / jnp.sum(p, axis=-1, keepdims=True))
            probs.append(p.astype(BF16))
        for hp in range(n_pairs):
            sl = slice(hp * LANES, (hp + 1) * LANES)
            o = _dot(probs[hp], vpad[pl.ds(start, K_TILE), sl]) * inv[hp]
            o_ref[0, pl.ds(start, Q_TILE), sl] = jnp.where(low, o[:Q_TILE], o[Q_TILE:]).astype(o_ref.dtype)
        return carry

    n_steps = s_len // Q_TILE
    n_early = min(A_REACH // Q_TILE, n_steps)
    lax.fori_loop(0, n_early, functools.partial(step, True), 0)
    lax.fori_loop(n_early, n_steps, functools.partial(step, False), 0)


def _attn_prompt(q, k, v, bias):
    b, s_len, w = q.shape
    assert s_len % 256 == 0
    seq = pl.BlockSpec((1, s_len, w), lambda i: (i, 0, 0))
    return pl.pallas_call(
        _attn_prompt_body,
        out_shape=jax.ShapeDtypeStruct((b, s_len, w), BF16),
        grid=(b,),
        in_specs=[seq, seq, seq, _resident(bias.shape)],
        out_specs=seq,
        scratch_shapes=[
            pltpu.VMEM((A_REACH + s_len, w), BF16),
            pltpu.VMEM((A_REACH + s_len, w), BF16),
        ],
        compiler_params=_params(1),
        name="band_attn_prompt",
    )(q, k, v, bias)


ATTN_SAMPLE_SEQS = 4


def _attn_sample_body(q_ref, k_ref, v_ref, ck_ref, cv_ref, bc_ref, bn_ref, o_ref):
    nb, t = q_ref.shape[0], q_ref.shape[1]
    low = _pair_masks(t)
    units = [(bi, hp) for bi in range(nb) for hp in range(N_HEADS_A // 2)]
    sc, sn = [], []
    for bi, hp in units:
        sl = slice(hp * LANES, (hp + 1) * LANES)
        qp = q_ref[bi, :, sl].astype(F32)
        lhs = jnp.concatenate([jnp.where(low, qp, 0.0), jnp.where(low, 0.0, qp)], axis=0).astype(BF16)
        sc.append(_dot_nt(lhs, ck_ref[bi, :, sl].astype(BF16)))
        sn.append(_dot_nt(lhs, k_ref[bi, :, sl].astype(BF16)))
    pc, pn, inv = [], [], []
    for (bi, hp), s_old, s_new in zip(units, sc, sn):
        s_old = s_old + bc_ref[hp]
        s_new = s_new + bn_ref[hp]
        m = jnp.maximum(jnp.max(s_old, axis=-1, keepdims=True), jnp.max(s_new, axis=-1, keepdims=True))
        e_old = jnp.exp2(s_old - m)
        e_new = jnp.exp2(s_new - m)
        inv.append(1.0 / (jnp.sum(e_old, axis=-1, keepdims=True) + jnp.sum(e_new, axis=-1, keepdims=True)))
        pc.append(e_old.astype(BF16))
        pn.append(e_new.astype(BF16))
    for (bi, hp), p_old, p_new, r in zip(units, pc, pn, inv):
        sl = slice(hp * LANES, (hp + 1) * LANES)
        o = (_dot(p_old, cv_ref[bi, :, sl].astype(BF16)) + _dot(p_new, v_ref[bi, :, sl].astype(BF16))) * r
        o_ref[bi, :, sl] = jnp.where(low, o[:t], o[t:]).astype(o_ref.dtype)


def _attn_sample(q, k, v, ck, cv, bias_cache, bias_new):
    b, t, w = q.shape
    p_len = ck.shape[1]
    nb = _token_tile(b, ATTN_SAMPLE_SEQS)
    new = pl.BlockSpec((nb, t, w), lambda i: (i, 0, 0))
    old = pl.BlockSpec((nb, p_len, w), lambda i: (i, 0, 0))
    return pl.pallas_call(
        _attn_sample_body,
        out_shape=jax.ShapeDtypeStruct((b, t, w), BF16),
        grid=(b // nb,),
        in_specs=[new, new, new, old, old, _resident(bias_cache.shape), _resident(bias_new.shape)],
        out_specs=new,
        compiler_params=_params(1),
        name="band_attn_sample",
    )(q, k, v, ck, cv, bias_cache, bias_new)


DELTA_BLOCK = 256
DELTA_GROUP = 16
DELTA_SEQS = 4


def _lane_blocks(a, b):
    z = jnp.zeros_like(a)
    return jnp.concatenate([jnp.concatenate([a, z], axis=-1), jnp.concatenate([z, b], axis=-1)], axis=0)


def _delta_body(*refs, valid_len, group, conv_here):
    if conv_here:
        xb_ref, bd_ref, prev_ref, s0_ref, cw_ref, alog_ref, dtb_ref, o_ref, conv_ref, s_ref, *scratch = refs
    else:
        xb_ref, bd_ref, s0_ref, alog_ref, dtb_ref, o_ref, s_ref, *scratch = refs
    s_scr, u_scr, wq_scr, ik_scr, gl_scr = scratch
    j = pl.program_id(1)
    nb, tb = xb_ref.shape[0], xb_ref.shape[1]
    L = CHUNK
    hd = HEAD_DIM_B
    assert hd == 2 * L and N_HEADS_B == 4
    n_chunks = tb // L
    n_pairs = N_HEADS_B // 2
    assert not conv_here or n_chunks == 1

    @pl.when(j == 0)
    def _():
        for bi in range(nb):
            for p in range(n_pairs):
                s_scr[bi, p] = jnp.concatenate([s0_ref[bi, 2 * p], s0_ref[bi, 2 * p + 1]], axis=-1)

    row = lax.broadcasted_iota(jnp.int32, (L, 2 * L), 0)
    lane = lax.broadcasted_iota(jnp.int32, (L, 2 * L), 1)
    col = lane & (L - 1)
    low = lane < L
    incl = col <= row
    strict = col < row
    eye = jnp.where(col == row, 1.0, 0.0).astype(F32)
    n_levels = L.bit_length() - 1
    level = [((row >> (k + 1)) == (col >> (k + 1))) & (((row >> k) & 1) == 1) & (((col >> k) & 1) == 0)
             for k in range(n_levels)]
    r64 = lax.broadcasted_iota(jnp.int32, (L, L), 0)
    c64 = lax.broadcasted_iota(jnp.int32, (L, L), 1)
    tril = jnp.where(c64 <= r64, 1.0, 0.0).astype(BF16)
    neg_a = -jnp.exp(alog_ref[...])
    dtb = dtb_ref[...]
    tok = lax.broadcasted_iota(jnp.int32, (L, 1), 0)

    def bd_pair(y):
        z = jnp.zeros_like(y)
        return jnp.concatenate([jnp.where(low, y, z), jnp.where(low, z, y)], axis=0)

    def pair_prod(x, y):
        return _dot(x.astype(BF16), bd_pair(y.astype(BF16)))

    def prep_chunk(bi, c):
        t0 = c * L if isinstance(c, int) else pl.multiple_of(c * L, L)
        xc = xb_ref[bi, pl.ds(t0, L), :]
        if conv_here:
            xc = _causal_conv_silu(prev_ref[bi], xc, cw_ref)
        bdc = bd_ref[bi, pl.ds(t0, L), :]
        beta_all = jax.nn.sigmoid(bdc)
        g_all = neg_a * _softplus(bdc + dtb)
        if valid_len is not None:
            ok = (tok + (j * tb + t0)) < valid_len
            beta_all = jnp.where(ok, beta_all, 0.0)
            g_all = jnp.where(ok, g_all, 0.0)
        g1 = g_all.astype(BF16)
        r1 = g_all - g1.astype(F32)
        g2 = r1.astype(BF16)
        g3 = (r1 - g2.astype(F32)).astype(BF16)
        gc_all = _dot(tril, g1) + _dot(tril, g2) + _dot(tril, g3)
        heads = []
        for h in range(N_HEADS_B):
            q = xc[:, h * hd:(h + 1) * hd]
            k = xc[:, WIDTH_B + h * hd:WIDTH_B + (h + 1) * hd]
            v = xc[:, 2 * WIDTH_B + h * hd:2 * WIDTH_B + (h + 1) * hd]
            q = q * lax.rsqrt(jnp.sum(q * q, axis=-1, keepdims=True) + EPS) * (hd ** -0.5)
            k = k * lax.rsqrt(jnp.sum(k * k, axis=-1, keepdims=True) + EPS)
            beta = jnp.broadcast_to(beta_all[:, h:h + 1], (L, hd))
            gc = jnp.broadcast_to(gc_all[:, N_HEADS_B + h:N_HEADS_B + h + 1], (L, hd))
            eg = jnp.exp(gc)
            g_last = gc[L - 1:L, :]
            kbeta = k * beta
            heads.append(dict(q=q, k=k, kbeta=kbeta, vbeta=v * beta, kbeg=kbeta * eg, qg=q * eg,
                              kdec=k * jnp.exp(g_last - gc), gc=gc, eg_last=jnp.exp(g_last)))
        return heads

    def phase1(jobs):
        heads = [prep_chunk(bi, c) for bi, c in jobs]
        units = [(g, p) for g in range(len(jobs)) for p in range(n_pairs)]
        gram = {}
        for (g, p) in units:
            a, b = heads[g][2 * p], heads[g][2 * p + 1]
            rhs = _lane_blocks(a["k"], b["k"]).astype(BF16)
            lhs = jnp.concatenate([jnp.concatenate([a["kbeta"], b["kbeta"]], axis=-1),
                                   jnp.concatenate([a["q"], b["q"]], axis=-1)], axis=0).astype(BF16)
            gram[g, p] = _dot_nt(lhs, rhs)
        a_mat, intra, t_inv = {}, {}, {}
        for (g, p) in units:
            a, b = heads[g][2 * p], heads[g][2 * p + 1]
            gc_i = jnp.where(low, a["gc"], b["gc"])
            gc_j = jnp.concatenate([a["gc"], b["gc"]], axis=0).T[0:L, :]
            decay = jnp.exp(jnp.where(incl, gc_i - gc_j, -jnp.inf))
            a_mat[g, p] = jnp.where(strict, gram[g, p][0:L] * decay, 0.0)
            intra[g, p] = gram[g, p][L:2 * L] * decay
            t_inv[g, p] = eye - jnp.where(level[0], a_mat[g, p], 0.0)
        for k in range(1, n_levels):
            y = {u: pair_prod(jnp.where(level[k], a_mat[u], 0.0), t_inv[u]) for u in units}
            t_inv = {u: t_inv[u] - pair_prod(t_inv[u], y[u]) for u in units}

        def solve_rhs(x, y):
            z = jnp.zeros((L, hd), BF16)
            return jnp.concatenate([jnp.concatenate([x[0:L], z, x[L:2 * L], z], axis=-1),
                                    jnp.concatenate([z, y[0:L], z, y[L:2 * L]], axis=-1)], axis=0)

        for (g, p) in units:
            a, b = heads[g][2 * p], heads[g][2 * p + 1]
            th, tl = _split2(t_inv[g, p])
            ah, al = _split2(jnp.concatenate([a["vbeta"], a["kbeg"]], axis=0))
            bh, bl = _split2(jnp.concatenate([b["vbeta"], b["kbeg"]], axis=0))
            rh, rl = solve_rhs(ah, bh), solve_rhs(al, bl)
            sol = _dot(th, rh) + _dot(th, rl) + _dot(tl, rh)
            bi, c = jobs[g]
            u_scr[bi, c, p] = sol[:, 0:2 * hd]
            wq_scr[bi, c, p] = jnp.concatenate(
                [sol[:, 2 * hd:4 * hd], jnp.concatenate([a["qg"], b["qg"]], axis=-1)], axis=0).astype(BF16)
            kdec_t = jnp.concatenate([a["kdec"], b["kdec"]], axis=0).T
            ik_scr[bi, c, p] = jnp.concatenate([intra[g, p], kdec_t], axis=0).astype(BF16)
            gl_scr[bi, c, p] = jnp.broadcast_to(
                jnp.concatenate([a["eg_last"], b["eg_last"]], axis=-1), (HIST_ROWS, 2 * hd))

    if nb * n_chunks <= DELTA_GROUP:
        phase1([(bi, c) for bi in range(nb) for c in range(n_chunks)])
    else:
        def group_of(bi, gi, carry):
            phase1([(bi, gi * group + g) for g in range(group)])
            return carry

        for bi in range(nb):
            lax.fori_loop(0, n_chunks // group, functools.partial(group_of, bi), 0)

    def phase2(c, carry):
        t0 = pl.multiple_of(c * L, L)
        units = [(bi, p) for bi in range(nb) for p in range(n_pairs)]
        r1, r2 = {}, {}
        for u in units:
            s = s_scr[u]
            r1[u] = _dot(wq_scr[u[0], c, u[1]], _lane_blocks(s[:, 0:hd], s[:, hd:2 * hd]).astype(BF16))
        for u in units:
            v_new = u_scr[u[0], c, u[1]] - r1[u][0:L]
            r2[u] = _dot(ik_scr[u[0], c, u[1]], _lane_blocks(v_new[:, 0:hd], v_new[:, hd:2 * hd]).astype(BF16))
        for u in units:
            bi, p = u
            o_ref[bi, pl.ds(t0, L), p * 2 * hd:(p + 1) * 2 * hd] = r1[u][L:2 * L] + r2[u][0:L]
            s_scr[u] = s_scr[u] * gl_scr[bi, c, p][0:1, :] + r2[u][L:L + hd]
        return carry

    lax.fori_loop(0, n_chunks, phase2, 0)

    @pl.when(j == pl.num_programs(1) - 1)
    def _():
        if conv_here:
            last = L if valid_len is None else valid_len
            conv_ref[...] = xb_ref[:, last - (CONV_W - 1):last, :]
        for bi in range(nb):
            for h in range(N_HEADS_B):
                s_ref[bi, h] = s_scr[bi, h // 2][:, (h % 2) * hd:(h % 2 + 1) * hd]


def _delta(xb, bd, s0, alog_row, dtb_row, prev=None, conv_w=None):
    b, t, cdim = xb.shape
    conv_here = conv_w is not None
    valid_len = None
    if t % CHUNK:
        assert CONV_W - 1 <= t < CHUNK
        valid_len = t
        xb = jnp.pad(xb, ((0, 0), (0, CHUNK - t), (0, 0)))
        bd = jnp.pad(bd, ((0, 0), (0, CHUNK - t), (0, 0)))
    tp = xb.shape[1]
    tb = _token_tile(tp, DELTA_BLOCK)
    n_chunks = tb // CHUNK
    group = _token_tile(n_chunks, DELTA_GROUP)
    n_pairs = N_HEADS_B // 2
    nb = _token_tile(b, DELTA_SEQS if n_chunks > 1 else DELTA_GROUP // 2)
    tok = lambda wdt: pl.BlockSpec((nb, tb, wdt), lambda i, j: (i, j, 0))
    state = pl.BlockSpec((nb, N_HEADS_B, HEAD_DIM_B, HEAD_DIM_B), lambda i, j: (i, 0, 0, 0))
    conv_state = pl.BlockSpec((nb, CONV_W - 1, cdim), lambda i, j: (i, 0, 0))
    o_shape = jax.ShapeDtypeStruct((b, tp, WIDTH_B), F32)
    s_shape = jax.ShapeDtypeStruct((b, N_HEADS_B, HEAD_DIM_B, HEAD_DIM_B), F32)
    if conv_here:
        assert tp == CHUNK
        prev = jnp.pad(prev, ((0, 0), (HIST_ROWS - (CONV_W - 1), 0), (0, 0)))
        args = (xb, bd, prev, s0, conv_w, alog_row, dtb_row)
        in_specs = [tok(cdim), tok(LANES), pl.BlockSpec((nb, HIST_ROWS, cdim), lambda i, j: (i, 0, 0)), state,
                    _resident(conv_w.shape), _resident((1, LANES)), _resident((1, LANES))]
        out_shape = (o_shape, jax.ShapeDtypeStruct((b, CONV_W - 1, cdim), F32), s_shape)
        out_specs = (tok(WIDTH_B), conv_state, state)
    else:
        args = (xb, bd, s0, alog_row, dtb_row)
        in_specs = [tok(cdim), tok(LANES), state, _resident((1, LANES)), _resident((1, LANES))]
        out_shape = (o_shape, s_shape)
        out_specs = (tok(WIDTH_B), state)
    outs = pl.pallas_call(
        functools.partial(_delta_body, valid_len=valid_len, group=group, conv_here=conv_here),
        out_shape=out_shape,
        grid=(b // nb, tp // tb),
        in_specs=in_specs,
        out_specs=out_specs,
        scratch_shapes=[
            pltpu.VMEM((nb, n_pairs, HEAD_DIM_B, 2 * HEAD_DIM_B), F32),
            pltpu.VMEM((nb, n_chunks, n_pairs, CHUNK, 2 * HEAD_DIM_B), F32),
            pltpu.VMEM((nb, n_chunks, n_pairs, 2 * CHUNK, 2 * HEAD_DIM_B), BF16),
            pltpu.VMEM((nb, n_chunks, n_pairs, CHUNK + HEAD_DIM_B, 2 * CHUNK), BF16),
            pltpu.VMEM((nb, n_chunks, n_pairs, HIST_ROWS, 2 * HEAD_DIM_B), F32),
        ],
        compiler_params=_params(2),
        name="gated_delta",
    )(*args)
    if conv_here:
        o, conv, s = outs
        return o[:, :t], conv, s
    o, s = outs
    return o[:, :t], s


def _memkv_body(m_ref, g_ref, wk_ref, wv_ref, k_ref, v_ref, k16_ref, v16_ref):
    h = _rms(m_ref[...], g_ref[...]).astype(BF16)
    k = _dot(h, wk_ref[...])
    v = _dot(h, wv_ref[...])
    k_ref[...] = k
    v_ref[...] = v
    k16_ref[...] = k.astype(BF16)
    v16_ref[...] = v.astype(BF16)


def _memkv(mem, g, wk, wv, tile=512):
    n, d = mem.shape
    tm = _token_tile(n, tile)
    tok = pl.BlockSpec((tm, d), lambda i: (i, 0))
    return pl.pallas_call(
        _memkv_body,
        out_shape=(jax.ShapeDtypeStruct((n, d), F32),) * 2 + (jax.ShapeDtypeStruct((n, d), BF16),) * 2,
        grid=(n // tm,),
        in_specs=[tok, _resident((1, d)), _resident(wk.shape), _resident(wv.shape)],
        out_specs=(tok,) * 4,
        compiler_params=_params(1),
        name="mem_kv",
    )(mem, g, wk, wv)


POST_ROWS = 128


def _post_body(x_ref, ya_ref, ob_ref, mk_ref, mv_ref, mixg_ref, on_ref, xg_ref,
               wz_ref, wgate_ref, wa_ref, wb_ref, wm_ref, wq_ref, wo_ref, o_ref):
    nb, tq, d = x_ref.shape
    x = x_ref[...].reshape(nb * tq, d)
    hd = HEAD_DIM_B
    h = _rms(x, mixg_ref[...]).astype(BF16)
    z = _dot(h, wz_ref[...])
    gate = jax.nn.sigmoid(_dot(h, wgate_ref[...]))
    ob = ob_ref[...].reshape(nb * tq, WIDTH_B)
    parts = []
    for i in range(N_HEADS_B):
        sl = slice(i * hd, (i + 1) * hd)
        parts.append(_rms(ob[:, sl], on_ref[...]) * _silu(z[:, sl]))
    obn = jnp.concatenate(parts, axis=-1).astype(BF16)
    ya = ya_ref[...].reshape(nb * tq, WIDTH_A)
    merged = gate[:, :d] * _dot(ya, wa_ref[...]) + gate[:, d:] * _dot(obn, wb_ref[...])
    x = x + _dot(merged.astype(BF16), wm_ref[...])

    hx = d // N_HEADS_X
    q = (_dot(_rms(x, xg_ref[...]).astype(BF16), wq_ref[...]) * (hx ** -0.5 * LOG2E)).astype(BF16)
    units = [(bi, slice(i * hx, (i + 1) * hx)) for bi in range(nb) for i in range(N_HEADS_X)]
    scores = [_dot_nt(q[bi * tq:(bi + 1) * tq, sl], mk_ref[bi, :, sl]) for bi, sl in units]
    probs, inv = [], []
    for s in scores:
        p = jnp.exp2(s - jnp.max(s, axis=-1, keepdims=True))
        inv.append(1.0 / jnp.sum(p, axis=-1, keepdims=True))
        probs.append(p.astype(BF16))
    outs = [_dot(p, mv_ref[bi, :, sl]) * r for p, r, (bi, sl) in zip(probs, inv, units)]
    rows = [jnp.concatenate(outs[bi * N_HEADS_X:(bi + 1) * N_HEADS_X], axis=-1) for bi in range(nb)]
    o = jnp.concatenate(rows, axis=0).astype(BF16)
    o_ref[...] = (x + _dot(o, wo_ref[...])).reshape(nb, tq, d)


def _post_mixer(x, ya, ob, mk, mv, lp, tile=512):
    b, t, d = x.shape
    n_mem = mk.shape[1]
    tq = _token_tile(t, tile)
    nb = _token_tile(b, max(1, POST_ROWS // tq))
    tok = lambda wdt: pl.BlockSpec((nb, tq, wdt), lambda i, j: (i, j, 0))
    mem = pl.BlockSpec((nb, n_mem, d), lambda i, j: (i, 0, 0))
    weights = [lp["mix_norm"], lp["b_out_norm"], lp["xattn_norm"], lp["w_z"], lp["w_gate"],
               lp["w_branch_a"], lp["w_branch_b"], lp["w_mix_out"], lp["xattn_wq"], lp["xattn_wo"]]
    return pl.pallas_call(
        _post_body,
        out_shape=jax.ShapeDtypeStruct((b, t, d), F32),
        grid=(b // nb, t // tq),
        in_specs=[tok(d), tok(WIDTH_A), tok(WIDTH_B), mem, mem] + [_resident(w.shape) for w in weights],
        out_specs=tok(d),
        compiler_params=_params(2),
        name="post_mixer",
    )(x, ya, ob, mk, mv, *weights)


def _rel_bias(table, q_len, k_first, n_keys):
    d_min = -(k_first + n_keys - 1)
    d_max = q_len - 1 - k_first
    core = table[:, max(d_min, -REL_CLIP) + REL_CLIP:min(d_max, REL_CLIP) + REL_CLIP + 1]
    ext = jnp.pad(core, ((0, 0), (max(0, -REL_CLIP - d_min), max(0, d_max - REL_CLIP))), mode="edge")
    rev = ext[:, ::-1]
    m = rev.shape[1]
    assert m == n_keys + q_len - 1 and q_len >= 2
    flat = jnp.tile(jnp.roll(rev, -(q_len - 1), axis=1), (1, q_len))[:, :q_len * (m - 1)]
    return flat.reshape(-1, q_len, m - 1)[:, :, :n_keys].astype(F32) * LOG2E


def _band_bias(table):
    t = jnp.arange(Q_TILE)
    kk = jnp.arange(K_TILE)
    bias = _rel_bias(table, Q_TILE, -A_REACH, K_TILE)
    q_chunk = t[:, None] // CHUNK
    k_chunk = kk[None, :] // CHUNK
    in_band = (k_chunk >= q_chunk) & (k_chunk <= q_chunk + A_REACH // CHUNK)
    bias = jnp.where(in_band[None], bias, -jnp.inf)
    return bias.reshape(N_HEADS_A // 2, 2 * Q_TILE, K_TILE)


def _trunk(x, lp, mem_k, mem_v, a_k_cache, a_v_cache, conv_prev, s0):
    b, t, d = x.shape
    n = b * t
    x = _ffn(x.reshape(n, d), lp["ffn1_norm"], lp["ffn1_wg"], lp["ffn1_wu"], lp["ffn1_wd"])
    if a_k_cache is None:
        qa, ka, va, xc, bd, tail = _proj(x, lp["mix_norm"], lp["w_cat"], conv_w=lp["b_conv_w"], seq_len=t)
        qa, ka, va = (a.reshape(b, t, WIDTH_A) for a in (qa, ka, va))
        ya = _attn_prompt(qa, ka, va, _band_bias(lp["a_rel_bias"]))
        keep = min(A_REACH, t)
        new_k, new_v = ka[:, t - keep:], va[:, t - keep:]
        ob, new_s = _delta(xc.reshape(b, t, B_CONV_DIM), bd.reshape(b, t, LANES), s0,
                           lp["alog_row"], lp["dtb_row"])
        new_conv = tail[:, HIST_ROWS - (CONV_W - 1):]
    else:
        qa, ka, va, xb, bd = _proj(x, lp["mix_norm"], lp["w_cat"])
        qa, ka, va = (a.reshape(b, t, WIDTH_A) for a in (qa, ka, va))
        p_len = a_k_cache.shape[1]
        ya = _attn_sample(
            qa, ka, va, a_k_cache.reshape(b, p_len, WIDTH_A), a_v_cache.reshape(b, p_len, WIDTH_A),
            _rel_bias(lp["a_rel_bias"], t, -p_len, p_len).reshape(N_HEADS_A // 2, 2 * t, p_len),
            _rel_bias(lp["a_rel_bias"], t, 0, t).reshape(N_HEADS_A // 2, 2 * t, t))
        new_k, new_v = ka, va
        ob, new_conv, new_s = _delta(xb.reshape(b, t, B_CONV_DIM), bd.reshape(b, t, LANES), s0,
                                     lp["alog_row"], lp["dtb_row"], prev=conv_prev, conv_w=lp["b_conv_w"])
    x = _post_mixer(x.reshape(b, t, d), ya, ob, mem_k, mem_v, lp)
    y = _ffn(x.reshape(n, d), lp["ffn2_norm"], lp["ffn2_wg"], lp["ffn2_wu"], lp["ffn2_wd"],
             final_g=lp["final_norm"])
    heads = (b, -1, N_HEADS_A, HEAD_DIM_A)
    return y.reshape(b, t, d), new_k.reshape(heads), new_v.reshape(heads), new_conv, new_s


def kernel(x_prompt, x_sample, cache_a_k, cache_a_v, state_b_conv, state_b_s, cache_mem_k, cache_mem_v, mem_prompt, ffn1_norm, ffn1_w_gate, ffn1_w_up, ffn1_w_down, mix_norm, w_in, a_rel_bias, b_conv_w, b_a_log, b_dt_bias, b_out_norm, w_branch_a, w_branch_b, w_mix_out, xattn_norm, mem_norm, xattn_wq, xattn_wk, xattn_wv, xattn_wo, ffn2_norm, ffn2_w_gate, ffn2_w_up, ffn2_w_down, final_norm):
    depth = ffn1_norm.shape[0]
    assert depth == 1
    l = 0
    d = x_prompt.shape[-1]
    bp = x_prompt.shape[0]
    n_mem = mem_prompt.shape[1]

    w = w_in[l]
    off_b = 3 * WIDTH_A
    off_z = off_b + B_CONV_DIM
    off_beta = off_z + WIDTH_B
    off_gate = off_beta + 2 * N_HEADS_B
    small = jnp.pad(w[:, off_beta:off_gate], ((0, 0), (0, LANES - 2 * N_HEADS_B)))
    w_cat = jnp.concatenate([w[:, :off_z], small], axis=1).astype(BF16)

    def lane_row(vec):
        return jnp.pad(vec.astype(F32), (N_HEADS_B, LANES - 2 * N_HEADS_B)).reshape(1, LANES)

    def row(vec):
        return vec.astype(F32).reshape(1, -1)

    lp = {
        "ffn1_norm": row(ffn1_norm[l]), "ffn1_wg": ffn1_w_gate[l].astype(BF16),
        "ffn1_wu": ffn1_w_up[l].astype(BF16), "ffn1_wd": ffn1_w_down[l].astype(BF16),
        "mix_norm": row(mix_norm[l]), "w_cat": w_cat, "w_z": w[:, off_z:off_beta].astype(BF16),
        "w_gate": w[:, off_gate:].astype(BF16), "a_rel_bias": a_rel_bias[l],
        "b_conv_w": b_conv_w[l], "alog_row": lane_row(b_a_log[l]), "dtb_row": lane_row(b_dt_bias[l]),
        "b_out_norm": row(b_out_norm[l]), "w_branch_a": w_branch_a[l].astype(BF16),
        "w_branch_b": w_branch_b[l].astype(BF16), "w_mix_out": w_mix_out[l].astype(BF16),
        "xattn_norm": row(xattn_norm[l]), "xattn_wq": xattn_wq[l].astype(BF16),
        "xattn_wo": xattn_wo[l].astype(BF16), "ffn2_norm": row(ffn2_norm[l]),
        "ffn2_wg": ffn2_w_gate[l].astype(BF16), "ffn2_wu": ffn2_w_up[l].astype(BF16),
        "ffn2_wd": ffn2_w_down[l].astype(BF16), "final_norm": row(final_norm),
    }

    mk_p, mv_p, mk16, mv16 = _memkv(mem_prompt.reshape(bp * n_mem, d), row(mem_norm[l]),
                                    xattn_wk[l].astype(BF16), xattn_wv[l].astype(BF16))
    s_zero = jnp.zeros((bp, N_HEADS_B, HEAD_DIM_B, HEAD_DIM_B), F32)
    yp, p_ak, p_av, p_cv, p_sb = _trunk(x_prompt, lp, mk16.reshape(bp, n_mem, d), mv16.reshape(bp, n_mem, d),
                                        None, None, None, s_zero)

    bs = x_sample.shape[0]
    ys, s_ak, s_av, s_cv, s_sb = _trunk(
        x_sample, lp, cache_mem_k[l].reshape(bs, n_mem, d).astype(BF16),
        cache_mem_v[l].reshape(bs, n_mem, d).astype(BF16),
        cache_a_k[l], cache_a_v[l], state_b_conv[l], state_b_s[l])

    mem_heads = (1, bp, n_mem, N_HEADS_X, d // N_HEADS_X)
    return (yp, ys, p_ak[None], p_av[None], p_cv[None], p_sb[None],
            mk_p.reshape(mem_heads), mv_p.reshape(mem_heads),
            s_ak[None], s_av[None], s_cv[None], s_sb[None])
```

```python
import functools

import jax
import jax.numpy as jnp
from jax import lax
from jax.experimental import pallas as pl
from jax.experimental.pallas import tpu as pltpu

F32 = jnp.float32
BF16 = jnp.bfloat16
EPS = 1e-6

CHUNK = 64
A_REACH = 8 * CHUNK
REL_CLIP = 128
N_HEADS_A = 8
HEAD_DIM_A = 64
WIDTH_A = N_HEADS_A * HEAD_DIM_A
N_HEADS_B = 4
HEAD_DIM_B = 128
WIDTH_B = N_HEADS_B * HEAD_DIM_B
CONV_W = 4
B_CONV_DIM = 3 * WIDTH_B
N_HEADS_X = 4
LANES = 128

VMEM_LIMIT_BYTES = 56 * 1024 * 1024


def _params(n_grid_axes):
    return pltpu.CompilerParams(
        dimension_semantics=("arbitrary",) * n_grid_axes,
        vmem_limit_bytes=VMEM_LIMIT_BYTES,
    )


def _resident(shape):
    nd = len(shape)
    return pl.BlockSpec(shape, lambda *_: (0,) * nd, pipeline_mode=pl.Buffered(1))


def _rms(x, g):
    return x * lax.rsqrt(jnp.mean(x * x, axis=-1, keepdims=True) + EPS) * g


def _silu(x):
    return x * jax.nn.sigmoid(x)


def _softplus(x):
    return jnp.maximum(x, 0.0) + jnp.log1p(jnp.exp(-jnp.abs(x)))


def _dot(a, b):
    return jnp.dot(a, b, preferred_element_type=F32)


def _dot_nt(a, b):
    return lax.dot_general(a, b, (((1,), (1,)), ((), ())), preferred_element_type=F32)


def _split2(x):
    hi = x.astype(BF16)
    lo = (x - hi.astype(F32)).astype(BF16)
    return hi, lo


def _token_tile(n, want):
    t = min(want, n)
    while n % t:
        t //= 2
    return t


FFN_PASS_COLS = 1536


def _ffn_body(*refs, final):
    if final:
        x_ref, g_ref, wg_ref, wu_ref, wd_ref, fg_ref, o_ref = refs
    else:
        x_ref, g_ref, wg_ref, wu_ref, wd_ref, o_ref = refs
    x = x_ref[...]
    h = _rms(x, g_ref[...]).astype(BF16)
    f = wg_ref.shape[1]
    cuts = [min(f, c * FFN_PASS_COLS) for c in range(-(-f // FFN_PASS_COLS) + 1)]
    acc = None
    for lo, hi in zip(cuts[:-1], cuts[1:]):
        a = (_silu(_dot(h, wg_ref[:, lo:hi])) * _dot(h, wu_ref[:, lo:hi])).astype(BF16)
        down = _dot(a, wd_ref[lo:hi, :])
        acc = down if acc is None else acc + down
    y = x + 0.5 * acc
    if final:
        y = _rms(y, fg_ref[...])
    o_ref[...] = y


def _ffn(x, g, wg, wu, wd, final_g=None, tile=1024):
    n, d = x.shape
    f = wg.shape[1]
    tm = _token_tile(n, tile)
    final = final_g is not None
    in_specs = [
        pl.BlockSpec((tm, d), lambda i: (i, 0)),
        _resident((1, d)),
        _resident((d, f)),
        _resident((d, f)),
        _resident((f, d)),
    ]
    args = [x, g, wg, wu, wd]
    if final:
        in_specs.append(_resident((1, d)))
        args.append(final_g)
    return pl.pallas_call(
        functools.partial(_ffn_body, final=final),
        out_shape=jax.ShapeDtypeStruct((n, d), F32),
        grid=(n // tm,),
        in_specs=in_specs,
        out_specs=pl.BlockSpec((tm, d), lambda i: (i, 0)),
        compiler_params=_params(1),
        name="ffn_final" if final else "ffn",
    )(*args)


LOG2E = 1.4426950408889634
Q_SCALE_A = HEAD_DIM_A ** -0.5 * LOG2E
HIST_ROWS = 8
CONV_ROWS = 64
_PROJ_WIDTHS = (WIDTH_A, WIDTH_A, WIDTH_A, B_CONV_DIM, LANES)


def _causal_conv_silu(before, cur, cw_ref):
    n = cur.shape[0]
    win = jnp.concatenate([before, cur], axis=0)
    y = cur * cw_ref[CONV_W - 1:CONV_W, :]
    for i in range(CONV_W - 1):
        lo = HIST_ROWS - (CONV_W - 1) + i
        y = y + win[lo:lo + n] * cw_ref[i:i + 1, :]
    return _silu(y)


def _proj_body(*refs, steps_per_seq):
    if steps_per_seq is None:
        x_ref, g_ref, w_ref, q_ref, k_ref, v_ref, xb_ref, bd_ref = refs
    else:
        x_ref, g_ref, w_ref, cw_ref, q_ref, k_ref, v_ref, xb_ref, bd_ref, tail_ref, raw = refs
    h = _rms(x_ref[...], g_ref[...]).astype(BF16)
    w = WIDTH_A
    if steps_per_seq is None:
        xb_ref[...] = _dot(h, w_ref[:, 3 * w:3 * w + B_CONV_DIM])
    else:
        tm = x_ref.shape[0]

        @pl.when(pl.program_id(0) % steps_per_seq == 0)
        def _():
            raw[0:HIST_ROWS, :] = jnp.zeros((HIST_ROWS, B_CONV_DIM), F32)

        raw[HIST_ROWS:HIST_ROWS + tm, :] = _dot(h, w_ref[:, 3 * w:3 * w + B_CONV_DIM])
        for r in range(0, tm, CONV_ROWS):
            xb_ref[r:r + CONV_ROWS, :] = _causal_conv_silu(
                raw[r:r + HIST_ROWS, :], raw[r + HIST_ROWS:r + HIST_ROWS + CONV_ROWS, :], cw_ref)
        tail = raw[tm:tm + HIST_ROWS, :]
        tail_ref[0] = tail
        raw[0:HIST_ROWS, :] = tail
    q_ref[...] = (_dot(h, w_ref[:, 0:w]) * Q_SCALE_A).astype(q_ref.dtype)
    k_ref[...] = _dot(h, w_ref[:, w:2 * w])
    v_ref[...] = _dot(h, w_ref[:, 2 * w:3 * w])
    bd_ref[...] = _dot(h, w_ref[:, 3 * w + B_CONV_DIM:])


def _proj(x, g, w_cat, conv_w=None, seq_len=None, tile=512):
    n, d = x.shape
    assert sum(_PROJ_WIDTHS) == w_cat.shape[1]
    tm = _token_tile(n, tile)
    tok = lambda wdt: pl.BlockSpec((tm, wdt), lambda i: (i, 0))
    out_shape = [jax.ShapeDtypeStruct((n, wdt), BF16 if i == 0 else F32) for i, wdt in enumerate(_PROJ_WIDTHS)]
    out_specs = [tok(wdt) for wdt in _PROJ_WIDTHS]
    in_specs = [tok(d), _resident((1, d)), _resident(w_cat.shape)]
    args = [x, g, w_cat]
    scratch = []
    steps_per_seq = None
    if conv_w is not None:
        assert seq_len % tm == 0 and tm % CONV_ROWS == 0
        steps_per_seq = seq_len // tm
        in_specs.append(_resident(conv_w.shape))
        args.append(conv_w)
        out_shape.append(jax.ShapeDtypeStruct((n // seq_len, HIST_ROWS, B_CONV_DIM), F32))
        out_specs.append(pl.BlockSpec((1, HIST_ROWS, B_CONV_DIM), lambda i: (i // steps_per_seq, 0, 0)))
        scratch.append(pltpu.VMEM((HIST_ROWS + tm, B_CONV_DIM), F32))
    return pl.pallas_call(
        functools.partial(_proj_body, steps_per_seq=steps_per_seq),
        out_shape=tuple(out_shape),
        grid=(n // tm,),
        in_specs=in_specs,
        out_specs=tuple(out_specs),
        scratch_shapes=scratch,
        compiler_params=_params(1),
        name="in_proj",
    )(*args)


def _pair_masks(rows):
    lane = lax.broadcasted_iota(jnp.int32, (rows, LANES), 1)
    return lane < HEAD_DIM_A


Q_TILE = 2 * CHUNK
K_TILE = A_REACH + Q_TILE


def _attn_prompt_body(q_ref, k_ref, v_ref, bias_ref, o_ref, kpad, vpad):
    s_len = q_ref.shape[1]
    n_pairs = N_HEADS_A // 2

    kpad[0:A_REACH, :] = jnp.zeros((A_REACH, WIDTH_A), BF16)
    vpad[0:A_REACH, :] = jnp.zeros((A_REACH, WIDTH_A), BF16)

    def fill(i, carry):
        r = pl.multiple_of(i * 256, 256)
        kpad[pl.ds(A_REACH + r, 256), :] = k_ref[0, pl.ds(r, 256), :].astype(BF16)
        vpad[pl.ds(A_REACH + r, 256), :] = v_ref[0, pl.ds(r, 256), :].astype(BF16)
        return carry

    lax.fori_loop(0, s_len // 256, fill, 0)

    low = _pair_masks(Q_TILE)
    kcol = lax.broadcasted_iota(jnp.int32, (1, K_TILE), 1)

    def step(reaches_before_start, it, carry):
        start = pl.multiple_of(it * Q_TILE, Q_TILE)
        exists = (kcol + start) >= A_REACH
        scores = []
        for hp in range(n_pairs):
            sl = slice(hp * LANES, (hp + 1) * LANES)
            qp = q_ref[0, pl.ds(start, Q_TILE), sl]
            zero = jnp.zeros_like(qp)
            lhs = jnp.concatenate([jnp.where(low, qp, zero), jnp.where(low, zero, qp)], axis=0)
            scores.append(_dot_nt(lhs, kpad[pl.ds(start, K_TILE), sl]))
        probs, inv = [], []
        for hp in range(n_pairs):
            s = scores[hp] + bias_ref[hp]
            if reaches_before_start:
                s = jnp.where(exists, s, -jnp.inf)
            m = jnp.max(s, axis=-1, keepdims=True)
            p = jnp.exp2(s - m)
            inv.append(1.0 / jnp.sum(p, axis=-1, keepdims=True))
            probs.append(p.astype(BF16))
        for hp in range(n_pairs):
            sl = slice(hp * LANES, (hp + 1) * LANES)
            o = _dot(probs[hp], vpad[pl.ds(start, K_TILE), sl]) * inv[hp]
            o_ref[0, pl.ds(start, Q_TILE), sl] = jnp.where(low, o[:Q_TILE], o[Q_TILE:]).astype(o_ref.dtype)
        return carry

    n_steps = s_len // Q_TILE
    n_early = min(A_REACH // Q_TILE, n_steps)
    lax.fori_loop(0, n_early, functools.partial(step, True), 0)
    lax.fori_loop(n_early, n_steps, functools.partial(step, False), 0)


def _attn_prompt(q, k, v, bias):
    b, s_len, w = q.shape
    assert s_len % 256 == 0
    seq = pl.BlockSpec((1, s_len, w), lambda i: (i, 0, 0))
    return pl.pallas_call(
        _attn_prompt_body,
        out_shape=jax.ShapeDtypeStruct((b, s_len, w), BF16),
        grid=(b,),
        in_specs=[seq, seq, seq, _resident(bias.shape)],
        out_specs=seq,
        scratch_shapes=[
            pltpu.VMEM((A_REACH + s_len, w), BF16),
            pltpu.VMEM((A_REACH + s_len, w), BF16),
        ],
        compiler_params=_params(1),
        name="band_attn_prompt",
    )(q, k, v, bias)


ATTN_SAMPLE_SEQS = 4


def _attn_sample_body(q_ref, k_ref, v_ref, ck_ref, cv_ref, bc_ref, bn_ref, o_ref):
    nb, t = q_ref.shape[0], q_ref.shape[1]
    low = _pair_masks(t)
    units = [(bi, hp) for bi in range(nb) for hp in range(N_HEADS_A // 2)]
    sc, sn = [], []
    for bi, hp in units:
        sl = slice(hp * LANES, (hp + 1) * LANES)
        qp = q_ref[bi, :, sl].astype(F32)
        lhs = jnp.concatenate([jnp.where(low, qp, 0.0), jnp.where(low, 0.0, qp)], axis=0).astype(BF16)
        sc.append(_dot_nt(lhs, ck_ref[bi, :, sl].astype(BF16)))
        sn.append(_dot_nt(lhs, k_ref[bi, :, sl].astype(BF16)))
    pc, pn, inv = [], [], []
    for (bi, hp), s_old, s_new in zip(units, sc, sn):
        s_old = s_old + bc_ref[hp]
        s_new = s_new + bn_ref[hp]
        m = jnp.maximum(jnp.max(s_old, axis=-1, keepdims=True), jnp.max(s_new, axis=-1, keepdims=True))
        e_old = jnp.exp2(s_old - m)
        e_new = jnp.exp2(s_new - m)
        inv.append(1.0 / (jnp.sum(e_old, axis=-1, keepdims=True) + jnp.sum(e_new, axis=-1, keepdims=True)))
        pc.append(e_old.astype(BF16))
        pn.append(e_new.astype(BF16))
    for (bi, hp), p_old, p_new, r in zip(units, pc, pn, inv):
        sl = slice(hp * LANES, (hp + 1) * LANES)
        o = (_dot(p_old, cv_ref[bi, :, sl].astype(BF16)) + _dot(p_new, v_ref[bi, :, sl].astype(BF16))) * r
        o_ref[bi, :, sl] = jnp.where(low, o[:t], o[t:]).astype(o_ref.dtype)


def _attn_sample(q, k, v, ck, cv, bias_cache, bias_new):
    b, t, w = q.shape
    p_len = ck.shape[1]
    nb = _token_tile(b, ATTN_SAMPLE_SEQS)
    new = pl.BlockSpec((nb, t, w), lambda i: (i, 0, 0))
    old = pl.BlockSpec((nb, p_len, w), lambda i: (i, 0, 0))
    return pl.pallas_call(
        _attn_sample_body,
        out_shape=jax.ShapeDtypeStruct((b, t, w), BF16),
        grid=(b // nb,),
        in_specs=[new, new, new, old, old, _resident(bias_cache.shape), _resident(bias_new.shape)],
        out_specs=new,
        compiler_params=_params(1),
        name="band_attn_sample",
    )(q, k, v, ck, cv, bias_cache, bias_new)


DELTA_BLOCK = 256
DELTA_GROUP = 16
DELTA_SEQS = 4


def _lane_blocks(a, b):
    z = jnp.zeros_like(a)
    return jnp.concatenate([jnp.concatenate([a, z], axis=-1), jnp.concatenate([z, b], axis=-1)], axis=0)


def _delta_body(*refs, valid_len, group, conv_here):
    if conv_here:
        xb_ref, bd_ref, prev_ref, s0_ref, cw_ref, alog_ref, dtb_ref, o_ref, conv_ref, s_ref, *scratch = refs
    else:
        xb_ref, bd_ref, s0_ref, alog_ref, dtb_ref, o_ref, s_ref, *scratch = refs
    s_scr, u_scr, wq_scr, ik_scr, gl_scr = scratch
    j = pl.program_id(1)
    nb, tb = xb_ref.shape[0], xb_ref.shape[1]
    L = CHUNK
    hd = HEAD_DIM_B
    assert hd == 2 * L and N_HEADS_B == 4
    n_chunks = tb // L
    n_pairs = N_HEADS_B // 2
    assert not conv_here or n_chunks == 1

    @pl.when(j == 0)
    def _():
        for bi in range(nb):
            for p in range(n_pairs):
                s_scr[bi, p] = jnp.concatenate([s0_ref[bi, 2 * p], s0_ref[bi, 2 * p + 1]], axis=-1)

    row = lax.broadcasted_iota(jnp.int32, (L, 2 * L), 0)
    lane = lax.broadcasted_iota(jnp.int32, (L, 2 * L), 1)
    col = lane & (L - 1)
    low = lane < L
    incl = col <= row
    strict = col < row
    eye = jnp.where(col == row, 1.0, 0.0).astype(F32)
    n_levels = L.bit_length() - 1
    level = [((row >> (k + 1)) == (col >> (k + 1))) & (((row >> k) & 1) == 1) & (((col >> k) & 1) == 0)
             for k in range(n_levels)]
    r64 = lax.broadcasted_iota(jnp.int32, (L, L), 0)
    c64 = lax.broadcasted_iota(jnp.int32, (L, L), 1)
    tril = jnp.where(c64 <= r64, 1.0, 0.0).astype(BF16)
    neg_a = -jnp.exp(alog_ref[...])
    dtb = dtb_ref[...]
    tok = lax.broadcasted_iota(jnp.int32, (L, 1), 0)

    def bd_pair(y):
        z = jnp.zeros_like(y)
        return jnp.concatenate([jnp.where(low, y, z), jnp.where(low, z, y)], axis=0)

    def pair_prod(x, y):
        return _dot(x.astype(BF16), bd_pair(y.astype(BF16)))

    def prep_chunk(bi, c):
        t0 = c * L if isinstance(c, int) else pl.multiple_of(c * L, L)
        xc = xb_ref[bi, pl.ds(t0, L), :]
        if conv_here:
            xc = _causal_conv_silu(prev_ref[bi], xc, cw_ref)
        bdc = bd_ref[bi, pl.ds(t0, L), :]
        beta_all = jax.nn.sigmoid(bdc)
        g_all = neg_a * _softplus(bdc + dtb)
        if valid_len is not None:
            ok = (tok + (j * tb + t0)) < valid_len
            beta_all = jnp.where(ok, beta_all, 0.0)
            g_all = jnp.where(ok, g_all, 0.0)
        g1 = g_all.astype(BF16)
        r1 = g_all - g1.astype(F32)
        g2 = r1.astype(BF16)
        g3 = (r1 - g2.astype(F32)).astype(BF16)
        gc_all = _dot(tril, g1) + _dot(tril, g2) + _dot(tril, g3)
        heads = []
        for h in range(N_HEADS_B):
            q = xc[:, h * hd:(h + 1) * hd]
            k = xc[:, WIDTH_B + h * hd:WIDTH_B + (h + 1) * hd]
            v = xc[:, 2 * WIDTH_B + h * hd:2 * WIDTH_B + (h + 1) * hd]
            q = q * lax.rsqrt(jnp.sum(q * q, axis=-1, keepdims=True) + EPS) * (hd ** -0.5)
            k = k * lax.rsqrt(jnp.sum(k * k, axis=-1, keepdims=True) + EPS)
            beta = jnp.broadcast_to(beta_all[:, h:h + 1], (L, hd))
            gc = jnp.broadcast_to(gc_all[:, N_HEADS_B + h:N_HEADS_B + h + 1], (L, hd))
            eg = jnp.exp(gc)
            g_last = gc[L - 1:L, :]
            kbeta = k * beta
            heads.append(dict(q=q, k=k, kbeta=kbeta, vbeta=v * beta, kbeg=kbeta * eg, qg=q * eg,
                              kdec=k * jnp.exp(g_last - gc), gc=gc, eg_last=jnp.exp(g_last)))
        return heads

    def phase1(jobs):
        heads = [prep_chunk(bi, c) for bi, c in jobs]
        units = [(g, p) for g in range(len(jobs)) for p in range(n_pairs)]
        gram = {}
        for (g, p) in units:
            a, b = heads[g][2 * p], heads[g][2 * p + 1]
            rhs = _lane_blocks(a["k"], b["k"]).astype(BF16)
            lhs = jnp.concatenate([jnp.concatenate([a["kbeta"], b["kbeta"]], axis=-1),
                                   jnp.concatenate([a["q"], b["q"]], axis=-1)], axis=0).astype(BF16)
            gram[g, p] = _dot_nt(lhs, rhs)
        a_mat, intra, t_inv = {}, {}, {}
        for (g, p) in units:
            a, b = heads[g][2 * p], heads[g][2 * p + 1]
            gc_i = jnp.where(low, a["gc"], b["gc"])
            gc_j = jnp.concatenate([a["gc"], b["gc"]], axis=0).T[0:L, :]
            decay = jnp.exp(jnp.where(incl, gc_i - gc_j, -jnp.inf))
            a_mat[g, p] = jnp.where(strict, gram[g, p][0:L] * decay, 0.0)
            intra[g, p] = gram[g, p][L:2 * L] * decay
            t_inv[g, p] = eye - jnp.where(level[0], a_mat[g, p], 0.0)
        for k in range(1, n_levels):
            y = {u: pair_prod(jnp.where(level[k], a_mat[u], 0.0), t_inv[u]) for u in units}
            t_inv = {u: t_inv[u] - pair_prod(t_inv[u], y[u]) for u in units}

        def solve_rhs(x, y):
            z = jnp.zeros((L, hd), BF16)
            return jnp.concatenate([jnp.concatenate([x[0:L], z, x[L:2 * L], z], axis=-1),
                                    jnp.concatenate([z, y[0:L], z, y[L:2 * L]], axis=-1)], axis=0)

        for (g, p) in units:
            a, b = heads[g][2 * p], heads[g][2 * p + 1]
            th, tl = _split2(t_inv[g, p])
            ah, al = _split2(jnp.concatenate([a["vbeta"], a["kbeg"]], axis=0))
            bh, bl = _split2(jnp.concatenate([b["vbeta"], b["kbeg"]], axis=0))
            rh, rl = solve_rhs(ah, bh), solve_rhs(al, bl)
            sol = _dot(th, rh) + _dot(th, rl) + _dot(tl, rh)
            bi, c = jobs[g]
            u_scr[bi, c, p] = sol[:, 0:2 * hd]
            wq_scr[bi, c, p] = jnp.concatenate(
                [sol[:, 2 * hd:4 * hd], jnp.concatenate([a["qg"], b["qg"]], axis=-1)], axis=0).astype(BF16)
            kdec_t = jnp.concatenate([a["kdec"], b["kdec"]], axis=0).T
            ik_scr[bi, c, p] = jnp.concatenate([intra[g, p], kdec_t], axis=0).astype(BF16)
            gl_scr[bi, c, p] = jnp.broadcast_to(
                jnp.concatenate([a["eg_last"], b["eg_last"]], axis=-1), (HIST_ROWS, 2 * hd))

    if nb * n_chunks <= DELTA_GROUP:
        phase1([(bi, c) for bi in range(nb) for c in range(n_chunks)])
    else:
        def group_of(bi, gi, carry):
            phase1([(bi, gi * group + g) for g in range(group)])
            return carry

        for bi in range(nb):
            lax.fori_loop(0, n_chunks // group, functools.partial(group_of, bi), 0)

    def phase2(c, carry):
        t0 = pl.multiple_of(c * L, L)
        units = [(bi, p) for bi in range(nb) for p in range(n_pairs)]
        r1, r2 = {}, {}
        for u in units:
            s = s_scr[u]
            r1[u] = _dot(wq_scr[u[0], c, u[1]], _lane_blocks(s[:, 0:hd], s[:, hd:2 * hd]).astype(BF16))
        for u in units:
            v_new = u_scr[u[0], c, u[1]] - r1[u][0:L]
            r2[u] = _dot(ik_scr[u[0], c, u[1]], _lane_blocks(v_new[:, 0:hd], v_new[:, hd:2 * hd]).astype(BF16))
        for u in units:
            bi, p = u
            o_ref[bi, pl.ds(t0, L), p * 2 * hd:(p + 1) * 2 * hd] = r1[u][L:2 * L] + r2[u][0:L]
            s_scr[u] = s_scr[u] * gl_scr[bi, c, p][0:1, :] + r2[u][L:L + hd]
        return carry

    lax.fori_loop(0, n_chunks, phase2, 0)

    @pl.when(j == pl.num_programs(1) - 1)
    def _():
        if conv_here:
            last = L if valid_len is None else valid_len
            conv_ref[...] = xb_ref[:, last - (CONV_W - 1):last, :]
        for bi in range(nb):
            for h in range(N_HEADS_B):
                s_ref[bi, h] = s_scr[bi, h // 2][:, (h % 2) * hd:(h % 2 + 1) * hd]


def _delta(xb, bd, s0, alog_row, dtb_row, prev=None, conv_w=None):
    b, t, cdim = xb.shape
    conv_here = conv_w is not None
    valid_len = None
    if t % CHUNK:
        assert CONV_W - 1 <= t < CHUNK
        valid_len = t
        xb = jnp.pad(xb, ((0, 0), (0, CHUNK - t), (0, 0)))
        bd = jnp.pad(bd, ((0, 0), (0, CHUNK - t), (0, 0)))
    tp = xb.shape[1]
    tb = _token_tile(tp, DELTA_BLOCK)
    n_chunks = tb // CHUNK
    group = _token_tile(n_chunks, DELTA_GROUP)
    n_pairs = N_HEADS_B // 2
    nb = _token_tile(b, DELTA_SEQS if n_chunks > 1 else DELTA_GROUP // 2)
    tok = lambda wdt: pl.BlockSpec((nb, tb, wdt), lambda i, j: (i, j, 0))
    state = pl.BlockSpec((nb, N_HEADS_B, HEAD_DIM_B, HEAD_DIM_B), lambda i, j: (i, 0, 0, 0))
    conv_state = pl.BlockSpec((nb, CONV_W - 1, cdim), lambda i, j: (i, 0, 0))
    o_shape = jax.ShapeDtypeStruct((b, tp, WIDTH_B), F32)
    s_shape = jax.ShapeDtypeStruct((b, N_HEADS_B, HEAD_DIM_B, HEAD_DIM_B), F32)
    if conv_here:
        assert tp == CHUNK
        prev = jnp.pad(prev, ((0, 0), (HIST_ROWS - (CONV_W - 1), 0), (0, 0)))
        args = (xb, bd, prev, s0, conv_w, alog_row, dtb_row)
        in_specs = [tok(cdim), tok(LANES), pl.BlockSpec((nb, HIST_ROWS, cdim), lambda i, j: (i, 0, 0)), state,
                    _resident(conv_w.shape), _resident((1, LANES)), _resident((1, LANES))]
        out_shape = (o_shape, jax.ShapeDtypeStruct((b, CONV_W - 1, cdim), F32), s_shape)
        out_specs = (tok(WIDTH_B), conv_state, state)
    else:
        args = (xb, bd, s0, alog_row, dtb_row)
        in_specs = [tok(cdim), tok(LANES), state, _resident((1, LANES)), _resident((1, LANES))]
        out_shape = (o_shape, s_shape)
        out_specs = (tok(WIDTH_B), state)
    outs = pl.pallas_call(
        functools.partial(_delta_body, valid_len=valid_len, group=group, conv_here=conv_here),
        out_shape=out_shape,
        grid=(b // nb, tp // tb),
        in_specs=in_specs,
        out_specs=out_specs,
        scratch_shapes=[
            pltpu.VMEM((nb, n_pairs, HEAD_DIM_B, 2 * HEAD_DIM_B), F32),
            pltpu.VMEM((nb, n_chunks, n_pairs, CHUNK, 2 * HEAD_DIM_B), F32),
            pltpu.VMEM((nb, n_chunks, n_pairs, 2 * CHUNK, 2 * HEAD_DIM_B), BF16),
            pltpu.VMEM((nb, n_chunks, n_pairs, CHUNK + HEAD_DIM_B, 2 * CHUNK), BF16),
            pltpu.VMEM((nb, n_chunks, n_pairs, HIST_ROWS, 2 * HEAD_DIM_B), F32),
        ],
        compiler_params=_params(2),
        name="gated_delta",
    )(*args)
    if conv_here:
        o, conv, s = outs
        return o[:, :t], conv, s
    o, s = outs
    return o[:, :t], s


def _memkv_body(m_ref, g_ref, wk_ref, wv_ref, k_ref, v_ref, k16_ref, v16_ref):
    h = _rms(m_ref[...], g_ref[...]).astype(BF16)
    k = _dot(h, wk_ref[...])
    v = _dot(h, wv_ref[...])
    k_ref[...] = k
    v_ref[...] = v
    k16_ref[...] = k.astype(BF16)
    v16_ref[...] = v.astype(BF16)


def _memkv(mem, g, wk, wv, tile=512):
    n, d = mem.shape
    tm = _token_tile(n, tile)
    tok = pl.BlockSpec((tm, d), lambda i: (i, 0))
    return pl.pallas_call(
        _memkv_body,
        out_shape=(jax.ShapeDtypeStruct((n, d), F32),) * 2 + (jax.ShapeDtypeStruct((n, d), BF16),) * 2,
        grid=(n // tm,),
        in_specs=[tok, _resident((1, d)), _resident(wk.shape), _resident(wv.shape)],
        out_specs=(tok,) * 4,
        compiler_params=_params(1),
        name="mem_kv",
    )(mem, g, wk, wv)


POST_ROWS = 128


def _post_body(x_ref, ya_ref, ob_ref, mk_ref, mv_ref, mixg_ref, on_ref, xg_ref,
               wz_ref, wgate_ref, wa_ref, wb_ref, wm_ref, wq_ref, wo_ref, o_ref):
    nb, tq, d = x_ref.shape
    x = x_ref[...].reshape(nb * tq, d)
    hd = HEAD_DIM_B
    h = _rms(x, mixg_ref[...]).astype(BF16)
    z = _dot(h, wz_ref[...])
    gate = jax.nn.sigmoid(_dot(h, wgate_ref[...]))
    ob = ob_ref[...].reshape(nb * tq, WIDTH_B)
    parts = []
    for i in range(N_HEADS_B):
        sl = slice(i * hd, (i + 1) * hd)
        parts.append(_rms(ob[:, sl], on_ref[...]) * _silu(z[:, sl]))
    obn = jnp.concatenate(parts, axis=-1).astype(BF16)
    ya = ya_ref[...].reshape(nb * tq, WIDTH_A)
    merged = gate[:, :d] * _dot(ya, wa_ref[...]) + gate[:, d:] * _dot(obn, wb_ref[...])
    x = x + _dot(merged.astype(BF16), wm_ref[...])

    hx = d // N_HEADS_X
    q = (_dot(_rms(x, xg_ref[...]).astype(BF16), wq_ref[...]) * (hx ** -0.5 * LOG2E)).astype(BF16)
    units = [(bi, slice(i * hx, (i + 1) * hx)) for bi in range(nb) for i in range(N_HEADS_X)]
    scores = [_dot_nt(q[bi * tq:(bi + 1) * tq, sl], mk_ref[bi, :, sl]) for bi, sl in units]
    probs, inv = [], []
    for s in scores:
        p = jnp.exp2(s - jnp.max(s, axis=-1, keepdims=True))
        inv.append(1.0 / jnp.sum(p, axis=-1, keepdims=True))
        probs.append(p.astype(BF16))
    outs = [_dot(p, mv_ref[bi, :, sl]) * r for p, r, (bi, sl) in zip(probs, inv, units)]
    rows = [jnp.concatenate(outs[bi * N_HEADS_X:(bi + 1) * N_HEADS_X], axis=-1) for bi in range(nb)]
    o = jnp.concatenate(rows, axis=0).astype(BF16)
    o_ref[...] = (x + _dot(o, wo_ref[...])).reshape(nb, tq, d)


def _post_mixer(x, ya, ob, mk, mv, lp, tile=512):
    b, t, d = x.shape
    n_mem = mk.shape[1]
    tq = _token_tile(t, tile)
    nb = _token_tile(b, max(1, POST_ROWS // tq))
    tok = lambda wdt: pl.BlockSpec((nb, tq, wdt), lambda i, j: (i, j, 0))
    mem = pl.BlockSpec((nb, n_mem, d), lambda i, j: (i, 0, 0))
    weights = [lp["mix_norm"], lp["b_out_norm"], lp["xattn_norm"], lp["w_z"], lp["w_gate"],
               lp["w_branch_a"], lp["w_branch_b"], lp["w_mix_out"], lp["xattn_wq"], lp["xattn_wo"]]
    return pl.pallas_call(
        _post_body,
        out_shape=jax.ShapeDtypeStruct((b, t, d), F32),
        grid=(b // nb, t // tq),
        in_specs=[tok(d), tok(WIDTH_A), tok(WIDTH_B), mem, mem] + [_resident(w.shape) for w in weights],
        out_specs=tok(d),
        compiler_params=_params(2),
        name="post_mixer",
    )(x, ya, ob, mk, mv, *weights)


def _rel_bias(table, q_len, k_first, n_keys):
    d_min = -(k_first + n_keys - 1)
    d_max = q_len - 1 - k_first
    core = table[:, max(d_min, -REL_CLIP) + REL_CLIP:min(d_max, REL_CLIP) + REL_CLIP + 1]
    ext = jnp.pad(core, ((0, 0), (max(0, -REL_CLIP - d_min), max(0, d_max - REL_CLIP))), mode="edge")
    rev = ext[:, ::-1]
    m = rev.shape[1]
    assert m == n_keys + q_len - 1 and q_len >= 2
    flat = jnp.tile(jnp.roll(rev, -(q_len - 1), axis=1), (1, q_len))[:, :q_len * (m - 1)]
    return flat.reshape(-1, q_len, m - 1)[:, :, :n_keys].astype(F32) * LOG2E


def _band_bias(table):
    t = jnp.arange(Q_TILE)
    kk = jnp.arange(K_TILE)
    bias = _rel_bias(table, Q_TILE, -A_REACH, K_TILE)
    q_chunk = t[:, None] // CHUNK
    k_chunk = kk[None, :] // CHUNK
    in_band = (k_chunk >= q_chunk) & (k_chunk <= q_chunk + A_REACH // CHUNK)
    bias = jnp.where(in_band[None], bias, -jnp.inf)
    return bias.reshape(N_HEADS_A // 2, 2 * Q_TILE, K_TILE)


def _trunk(x, lp, mem_k, mem_v, a_k_cache, a_v_cache, conv_prev, s0):
    b, t, d = x.shape
    n = b * t
    x = _ffn(x.reshape(n, d), lp["ffn1_norm"], lp["ffn1_wg"], lp["ffn1_wu"], lp["ffn1_wd"])
    if a_k_cache is None:
        qa, ka, va, xc, bd, tail = _proj(x, lp["mix_norm"], lp["w_cat"], conv_w=lp["b_conv_w"], seq_len=t)
        qa, ka, va = (a.reshape(b, t, WIDTH_A) for a in (qa, ka, va))
        ya = _attn_prompt(qa, ka, va, _band_bias(lp["a_rel_bias"]))
        keep = min(A_REACH, t)
        new_k, new_v = ka[:, t - keep:], va[:, t - keep:]
        ob, new_s = _delta(xc.reshape(b, t, B_CONV_DIM), bd.reshape(b, t, LANES), s0,
                           lp["alog_row"], lp["dtb_row"])
        new_conv = tail[:, HIST_ROWS - (CONV_W - 1):]
    else:
        qa, ka, va, xb, bd = _proj(x, lp["mix_norm"], lp["w_cat"])
        qa, ka, va = (a.reshape(b, t, WIDTH_A) for a in (qa, ka, va))
        p_len = a_k_cache.shape[1]
        ya = _attn_sample(
            qa, ka, va, a_k_cache.reshape(b, p_len, WIDTH_A), a_v_cache.reshape(b, p_len, WIDTH_A),
            _rel_bias(lp["a_rel_bias"], t, -p_len, p_len).reshape(N_HEADS_A // 2, 2 * t, p_len),
            _rel_bias(lp["a_rel_bias"], t, 0, t).reshape(N_HEADS_A // 2, 2 * t, t))
        new_k, new_v = ka, va
        ob, new_conv, new_s = _delta(xb.reshape(b, t, B_CONV_DIM), bd.reshape(b, t, LANES), s0,
                                     lp["alog_row"], lp["dtb_row"], prev=conv_prev, conv_w=lp["b_conv_w"])
    x = _post_mixer(x.reshape(b, t, d), ya, ob, mem_k, mem_v, lp)
    y = _ffn(x.reshape(n, d), lp["ffn2_norm"], lp["ffn2_wg"], lp["ffn2_wu"], lp["ffn2_wd"],
             final_g=lp["final_norm"])
    heads = (b, -1, N_HEADS_A, HEAD_DIM_A)
    return y.reshape(b, t, d), new_k.reshape(heads), new_v.reshape(heads), new_conv, new_s


def kernel(x_prompt, x_sample, cache_a_k, cache_a_v, state_b_conv, state_b_s, cache_mem_k, cache_mem_v, mem_prompt, ffn1_norm, ffn1_w_gate, ffn1_w_up, ffn1_w_down, mix_norm, w_in, a_rel_bias, b_conv_w, b_a_log, b_dt_bias, b_out_norm, w_branch_a, w_branch_b, w_mix_out, xattn_norm, mem_norm, xattn_wq, xattn_wk, xattn_wv, xattn_wo, ffn2_norm, ffn2_w_gate, ffn2_w_up, ffn2_w_down, final_norm):
    depth = ffn1_norm.shape[0]
    assert depth == 1
    l = 0
    d = x_prompt.shape[-1]
    bp = x_prompt.shape[0]
    n_mem = mem_prompt.shape[1]

    w = w_in[l]
    off_b = 3 * WIDTH_A
    off_z = off_b + B_CONV_DIM
    off_beta = off_z + WIDTH_B
    off_gate = off_beta + 2 * N_HEADS_B
    small = jnp.pad(w[:, off_beta:off_gate], ((0, 0), (0, LANES - 2 * N_HEADS_B)))
    w_cat = jnp.concatenate([w[:, :off_z], small], axis=1).astype(BF16)

    def lane_row(vec):
        return jnp.pad(vec.astype(F32), (N_HEADS_B, LANES - 2 * N_HEADS_B)).reshape(1, LANES)

    def row(vec):
        return vec.astype(F32).reshape(1, -1)

    lp = {
        "ffn1_norm": row(ffn1_norm[l]), "ffn1_wg": ffn1_w_gate[l].astype(BF16),
        "ffn1_wu": ffn1_w_up[l].astype(BF16), "ffn1_wd": ffn1_w_down[l].astype(BF16),
        "mix_norm": row(mix_norm[l]), "w_cat": w_cat, "w_z": w[:, off_z:off_beta].astype(BF16),
        "w_gate": w[:, off_gate:].astype(BF16), "a_rel_bias": a_rel_bias[l],
        "b_conv_w": b_conv_w[l], "alog_row": lane_row(b_a_log[l]), "dtb_row": lane_row(b_dt_bias[l]),
        "b_out_norm": row(b_out_norm[l]), "w_branch_a": w_branch_a[l].astype(BF16),
        "w_branch_b": w_branch_b[l].astype(BF16), "w_mix_out": w_mix_out[l].astype(BF16),
        "xattn_norm": row(xattn_norm[l]), "xattn_wq": xattn_wq[l].astype(BF16),
        "xattn_wo": xattn_wo[l].astype(BF16), "ffn2_norm": row(ffn2_norm[l]),
        "ffn2_wg": ffn2_w_gate[l].astype(BF16), "ffn2_wu": ffn2_w_up[l].astype(BF16),
        "ffn2_wd": ffn2_w_down[l].astype(BF16), "final_norm": row(final_norm),
    }

    mk_p, mv_p, mk16, mv16 = _memkv(mem_prompt.reshape(bp * n_mem, d), row(mem_norm[l]),
                                    xattn_wk[l].astype(BF16), xattn_wv[l].astype(BF16))
    s_zero = jnp.zeros((bp, N_HEADS_B, HEAD_DIM_B, HEAD_DIM_B), F32)
    yp, p_ak, p_av, p_cv, p_sb = _trunk(x_prompt, lp, mk16.reshape(bp, n_mem, d), mv16.reshape(bp, n_mem, d),
                                        None, None, None, s_zero)

    bs = x_sample.shape[0]
    ys, s_ak, s_av, s_cv, s_sb = _trunk(
        x_sample, lp, cache_mem_k[l].reshape(bs, n_mem, d).astype(BF16),
        cache_mem_v[l].reshape(bs, n_mem, d).astype(BF16),
        cache_a_k[l], cache_a_v[l], state_b_conv[l], state_b_s[l])

    mem_heads = (1, bp, n_mem, N_HEADS_X, d // N_HEADS_X)
    return (yp, ys, p_ak[None], p_av[None], p_cv[None], p_sb[None],
            mk_p.reshape(mem_heads), mv_p.reshape(mem_heads),
            s_ak[None], s_av[None], s_cv[None], s_sb[None])
```

```python
import functools

import jax
import jax.numpy as jnp
from jax import lax
from jax.experimental import pallas as pl
from jax.experimental.pallas import tpu as pltpu

F32 = jnp.float32
BF16 = jnp.bfloat16
EPS = 1e-6

CHUNK = 64
A_REACH = 8 * CHUNK
REL_CLIP = 128
N_HEADS_A = 8
HEAD_DIM_A = 64
WIDTH_A = N_HEADS_A * HEAD_DIM_A
N_HEADS_B = 4
HEAD_DIM_B = 128
WIDTH_B = N_HEADS_B * HEAD_DIM_B
CONV_W = 4
B_CONV_DIM = 3 * WIDTH_B
N_HEADS_X = 4
LANES = 128

VMEM_LIMIT_BYTES = 56 * 1024 * 1024


def _params(n_grid_axes):
    return pltpu.CompilerParams(
        dimension_semantics=("arbitrary",) * n_grid_axes,
        vmem_limit_bytes=VMEM_LIMIT_BYTES,
    )


def _resident(shape):
    nd = len(shape)
    return pl.BlockSpec(shape, lambda *_: (0,) * nd, pipeline_mode=pl.Buffered(1))


def _rms(x, g):
    return x * lax.rsqrt(jnp.mean(x * x, axis=-1, keepdims=True) + EPS) * g


def _silu(x):
    return x * jax.nn.sigmoid(x)


def _softplus(x):
    return jnp.maximum(x, 0.0) + jnp.log1p(jnp.exp(-jnp.abs(x)))


def _dot(a, b):
    return jnp.dot(a, b, preferred_element_type=F32)


def _dot_nt(a, b):
    return lax.dot_general(a, b, (((1,), (1,)), ((), ())), preferred_element_type=F32)


def _split2(x):
    hi = x.astype(BF16)
    lo = (x - hi.astype(F32)).astype(BF16)
    return hi, lo


def _token_tile(n, want):
    t = min(want, n)
    while n % t:
        t //= 2
    return t


FFN_PASS_COLS = 1536


def _ffn_body(*refs, final):
    if final:
        x_ref, g_ref, wg_ref, wu_ref, wd_ref, fg_ref, o_ref = refs
    else:
        x_ref, g_ref, wg_ref, wu_ref, wd_ref, o_ref = refs
    x = x_ref[...]
    h = _rms(x, g_ref[...]).astype(BF16)
    f = wg_ref.shape[1]
    cuts = [min(f, c * FFN_PASS_COLS) for c in range(-(-f // FFN_PASS_COLS) + 1)]
    acc = None
    for lo, hi in zip(cuts[:-1], cuts[1:]):
        a = (_silu(_dot(h, wg_ref[:, lo:hi])) * _dot(h, wu_ref[:, lo:hi])).astype(BF16)
        down = _dot(a, wd_ref[lo:hi, :])
        acc = down if acc is None else acc + down
    y = x + 0.5 * acc
    if final:
        y = _rms(y, fg_ref[...])
    o_ref[...] = y


def _ffn(x, g, wg, wu, wd, final_g=None, tile=1024):
    n, d = x.shape
    f = wg.shape[1]
    tm = _token_tile(n, tile)
    final = final_g is not None
    in_specs = [
        pl.BlockSpec((tm, d), lambda i: (i, 0)),
        _resident((1, d)),
        _resident((d, f)),
        _resident((d, f)),
        _resident((f, d)),
    ]
    args = [x, g, wg, wu, wd]
    if final:
        in_specs.append(_resident((1, d)))
        args.append(final_g)
    return pl.pallas_call(
        functools.partial(_ffn_body, final=final),
        out_shape=jax.ShapeDtypeStruct((n, d), F32),
        grid=(n // tm,),
        in_specs=in_specs,
        out_specs=pl.BlockSpec((tm, d), lambda i: (i, 0)),
        compiler_params=_params(1),
        name="ffn_final" if final else "ffn",
    )(*args)


LOG2E = 1.4426950408889634
Q_SCALE_A = HEAD_DIM_A ** -0.5 * LOG2E
HIST_ROWS = 8
CONV_ROWS = 64
_PROJ_WIDTHS = (WIDTH_A, WIDTH_A, WIDTH_A, B_CONV_DIM, LANES)


def _causal_conv_silu(before, cur, cw_ref):
    n = cur.shape[0]
    win = jnp.concatenate([before, cur], axis=0)
    y = cur * cw_ref[CONV_W - 1:CONV_W, :]
    for i in range(CONV_W - 1):
        lo = HIST_ROWS - (CONV_W - 1) + i
        y = y + win[lo:lo + n] * cw_ref[i:i + 1, :]
    return _silu(y)


def _proj_body(*refs, steps_per_seq):
    if steps_per_seq is None:
        x_ref, g_ref, w_ref, q_ref, k_ref, v_ref, xb_ref, bd_ref = refs
    else:
        x_ref, g_ref, w_ref, cw_ref, q_ref, k_ref, v_ref, xb_ref, bd_ref, tail_ref, raw = refs
    h = _rms(x_ref[...], g_ref[...]).astype(BF16)
    w = WIDTH_A
    if steps_per_seq is None:
        xb_ref[...] = _dot(h, w_ref[:, 3 * w:3 * w + B_CONV_DIM])
    else:
        tm = x_ref.shape[0]

        @pl.when(pl.program_id(0) % steps_per_seq == 0)
        def _():
            raw[0:HIST_ROWS, :] = jnp.zeros((HIST_ROWS, B_CONV_DIM), F32)

        raw[HIST_ROWS:HIST_ROWS + tm, :] = _dot(h, w_ref[:, 3 * w:3 * w + B_CONV_DIM])
        for r in range(0, tm, CONV_ROWS):
            xb_ref[r:r + CONV_ROWS, :] = _causal_conv_silu(
                raw[r:r + HIST_ROWS, :], raw[r + HIST_ROWS:r + HIST_ROWS + CONV_ROWS, :], cw_ref)
        tail = raw[tm:tm + HIST_ROWS, :]
        tail_ref[0] = tail
        raw[0:HIST_ROWS, :] = tail
    q_ref[...] = (_dot(h, w_ref[:, 0:w]) * Q_SCALE_A).astype(q_ref.dtype)
    k_ref[...] = _dot(h, w_ref[:, w:2 * w])
    v_ref[...] = _dot(h, w_ref[:, 2 * w:3 * w])
    bd_ref[...] = _dot(h, w_ref[:, 3 * w + B_CONV_DIM:])


def _proj(x, g, w_cat, conv_w=None, seq_len=None, tile=1024):
    n, d = x.shape
    assert sum(_PROJ_WIDTHS) == w_cat.shape[1]
    tm = _token_tile(n, tile)
    tok = lambda wdt: pl.BlockSpec((tm, wdt), lambda i: (i, 0))
    out_shape = [jax.ShapeDtypeStruct((n, wdt), BF16 if i == 0 else F32) for i, wdt in enumerate(_PROJ_WIDTHS)]
    out_specs = [tok(wdt) for wdt in _PROJ_WIDTHS]
    in_specs = [tok(d), _resident((1, d)), _resident(w_cat.shape)]
    args = [x, g, w_cat]
    scratch = []
    steps_per_seq = None
    if conv_w is not None:
        assert seq_len % tm == 0 and tm % CONV_ROWS == 0
        steps_per_seq = seq_len // tm
        in_specs.append(_resident(conv_w.shape))
        args.append(conv_w)
        out_shape.append(jax.ShapeDtypeStruct((n // seq_len, HIST_ROWS, B_CONV_DIM), F32))
        out_specs.append(pl.BlockSpec((1, HIST_ROWS, B_CONV_DIM), lambda i: (i // steps_per_seq, 0, 0)))
        scratch.append(pltpu.VMEM((HIST_ROWS + tm, B_CONV_DIM), F32))
    return pl.pallas_call(
        functools.partial(_proj_body, steps_per_seq=steps_per_seq),
        out_shape=tuple(out_shape),
        grid=(n // tm,),
        in_specs=in_specs,
        out_specs=tuple(out_specs),
        scratch_shapes=scratch,
        compiler_params=_params(1),
        name="in_proj",
    )(*args)


def _pair_masks(rows):
    lane = lax.broadcasted_iota(jnp.int32, (rows, LANES), 1)
    return lane < HEAD_DIM_A


Q_TILE = 2 * CHUNK
K_TILE = A_REACH + Q_TILE


def _attn_prompt_body(q_ref, k_ref, v_ref, bias_ref, o_ref, kpad, vpad):
    s_len = q_ref.shape[1]
    n_pairs = N_HEADS_A // 2

    kpad[0:A_REACH, :] = jnp.zeros((A_REACH, WIDTH_A), BF16)
    vpad[0:A_REACH, :] = jnp.zeros((A_REACH, WIDTH_A), BF16)

    def fill(i, carry):
        r = pl.multiple_of(i * 256, 256)
        kpad[pl.ds(A_REACH + r, 256), :] = k_ref[0, pl.ds(r, 256), :].astype(BF16)
        vpad[pl.ds(A_REACH + r, 256), :] = v_ref[0, pl.ds(r, 256), :].astype(BF16)
        return carry

    lax.fori_loop(0, s_len // 256, fill, 0)

    low = _pair_masks(Q_TILE)
    kcol = lax.broadcasted_iota(jnp.int32, (1, K_TILE), 1)

    def step(reaches_before_start, it, carry):
        start = pl.multiple_of(it * Q_TILE, Q_TILE)
        exists = (kcol + start) >= A_REACH
        scores = []
        for hp in range(n_pairs):
            sl = slice(hp * LANES, (hp + 1) * LANES)
            qp = q_ref[0, pl.ds(start, Q_TILE), sl]
            zero = jnp.zeros_like(qp)
            lhs = jnp.concatenate([jnp.where(low, qp, zero), jnp.where(low, zero, qp)], axis=0)
            scores.append(_dot_nt(lhs, kpad[pl.ds(start, K_TILE), sl]))
        probs, inv = [], []
        for hp in range(n_pairs):
            s = scores[hp] + bias_ref[hp]
            if reaches_before_start:
                s = jnp.where(exists, s, -jnp.inf)
            m = jnp.max(s, axis=-1, keepdims=True)
            p = jnp.exp2(s - m)
            inv.append(1.0 / jnp.sum(p, axis=-1, keepdims=True))
            probs.append(p.astype(BF16))
        for hp in range(n_pairs):
            sl = slice(hp * LANES, (hp + 1) * LANES)
            o = _dot(probs[hp], vpad[pl.ds(start, K_TILE), sl]) * inv[hp]
            o_ref[0, pl.ds(start, Q_TILE), sl] = jnp.where(low, o[:Q_TILE], o[Q_TILE:]).astype(o_ref.dtype)
        return carry

    n_steps = s_len // Q_TILE
    n_early = min(A_REACH // Q_TILE, n_steps)
    lax.fori_loop(0, n_early, functools.partial(step, True), 0)
    lax.fori_loop(n_early, n_steps, functools.partial(step, False), 0)


def _attn_prompt(q, k, v, bias):
    b, s_len, w = q.shape
    assert s_len % 256 == 0
    seq = pl.BlockSpec((1, s_len, w), lambda i: (i, 0, 0))
    return pl.pallas_call(
        _attn_prompt_body,
        out_shape=jax.ShapeDtypeStruct((b, s_len, w), BF16),
        grid=(b,),
        in_specs=[seq, seq, seq, _resident(bias.shape)],
        out_specs=seq,
        scratch_shapes=[
            pltpu.VMEM((A_REACH + s_len, w), BF16),
            pltpu.VMEM((A_REACH + s_len, w), BF16),
        ],
        compiler_params=_params(1),
        name="band_attn_prompt",
    )(q, k, v, bias)


ATTN_SAMPLE_SEQS = 4


def _attn_sample_body(q_ref, k_ref, v_ref, ck_ref, cv_ref, bc_ref, bn_ref, o_ref):
    nb, t = q_ref.shape[0], q_ref.shape[1]
    low = _pair_masks(t)
    units = [(bi, hp) for bi in range(nb) for hp in range(N_HEADS_A // 2)]
    sc, sn = [], []
    for bi, hp in units:
        sl = slice(hp * LANES, (hp + 1) * LANES)
        qp = q_ref[bi, :, sl].astype(F32)
        lhs = jnp.concatenate([jnp.where(low, qp, 0.0), jnp.where(low, 0.0, qp)], axis=0).astype(BF16)
        sc.append(_dot_nt(lhs, ck_ref[bi, :, sl].astype(BF16)))
        sn.append(_dot_nt(lhs, k_ref[bi, :, sl].astype(BF16)))
    pc, pn, inv = [], [], []
    for (bi, hp), s_old, s_new in zip(units, sc, sn):
        s_old = s_old + bc_ref[hp]
        s_new = s_new + bn_ref[hp]
        m = jnp.maximum(jnp.max(s_old, axis=-1, keepdims=True), jnp.max(s_new, axis=-1, keepdims=True))
        e_old = jnp.exp2(s_old - m)
        e_new = jnp.exp2(s_new - m)
        inv.append(1.0 / (jnp.sum(e_old, axis=-1, keepdims=True) + jnp.sum(e_new, axis=-1, keepdims=True)))
        pc.append(e_old.astype(BF16))
        pn.append(e_new.astype(BF16))
    for (bi, hp), p_old, p_new, r in zip(units, pc, pn, inv):
        sl = slice(hp * LANES, (hp + 1) * LANES)
        o = (_dot(p_old, cv_ref[bi, :, sl].astype(BF16)) + _dot(p_new, v_ref[bi, :, sl].astype(BF16))) * r
        o_ref[bi, :, sl] = jnp.where(low, o[:t], o[t:]).astype(o_ref.dtype)


def _attn_sample(q, k, v, ck, cv, bias_cache, bias_new):
    b, t, w = q.shape
    p_len = ck.shape[1]
    nb = _token_tile(b, ATTN_SAMPLE_SEQS)
    new = pl.BlockSpec((nb, t, w), lambda i: (i, 0, 0))
    old = pl.BlockSpec((nb, p_len, w), lambda i: (i, 0, 0))
    return pl.pallas_call(
        _attn_sample_body,
        out_shape=jax.ShapeDtypeStruct((b, t, w), BF16),
        grid=(b // nb,),
        in_specs=[new, new, new, old, old, _resident(bias_cache.shape), _resident(bias_new.shape)],
        out_specs=new,
        compiler_params=_params(1),
        name="band_attn_sample",
    )(q, k, v, ck, cv, bias_cache, bias_new)


DELTA_BLOCK = 256
DELTA_GROUP = 16
DELTA_SEQS = 4


def _lane_blocks(a, b):
    z = jnp.zeros_like(a)
    return jnp.concatenate([jnp.concatenate([a, z], axis=-1), jnp.concatenate([z, b], axis=-1)], axis=0)


def _delta_body(*refs, valid_len, group, conv_here):
    if conv_here:
        xb_ref, bd_ref, prev_ref, s0_ref, cw_ref, alog_ref, dtb_ref, o_ref, conv_ref, s_ref, *scratch = refs
    else:
        xb_ref, bd_ref, s0_ref, alog_ref, dtb_ref, o_ref, s_ref, *scratch = refs
    s_scr, u_scr, wq_scr, ik_scr, gl_scr = scratch
    j = pl.program_id(1)
    nb, tb = xb_ref.shape[0], xb_ref.shape[1]
    L = CHUNK
    hd = HEAD_DIM_B
    assert hd == 2 * L and N_HEADS_B == 4
    n_chunks = tb // L
    n_pairs = N_HEADS_B // 2
    assert not conv_here or n_chunks == 1

    @pl.when(j == 0)
    def _():
        for bi in range(nb):
            for p in range(n_pairs):
                s_scr[bi, p] = jnp.concatenate([s0_ref[bi, 2 * p], s0_ref[bi, 2 * p + 1]], axis=-1)

    row = lax.broadcasted_iota(jnp.int32, (L, 2 * L), 0)
    lane = lax.broadcasted_iota(jnp.int32, (L, 2 * L), 1)
    col = lane & (L - 1)
    low = lane < L
    incl = col <= row
    strict = col < row
    eye = jnp.where(col == row, 1.0, 0.0).astype(F32)
    n_levels = L.bit_length() - 1
    level = [((row >> (k + 1)) == (col >> (k + 1))) & (((row >> k) & 1) == 1) & (((col >> k) & 1) == 0)
             for k in range(n_levels)]
    r64 = lax.broadcasted_iota(jnp.int32, (L, L), 0)
    c64 = lax.broadcasted_iota(jnp.int32, (L, L), 1)
    tril = jnp.where(c64 <= r64, 1.0, 0.0).astype(BF16)
    neg_a = -jnp.exp(alog_ref[...])
    dtb = dtb_ref[...]
    tok = lax.broadcasted_iota(jnp.int32, (L, 1), 0)

    def bd_pair(y):
        z = jnp.zeros_like(y)
        return jnp.concatenate([jnp.where(low, y, z), jnp.where(low, z, y)], axis=0)

    def pair_prod(x, y):
        return _dot(x.astype(BF16), bd_pair(y.astype(BF16)))

    def prep_chunk(bi, c):
        t0 = c * L if isinstance(c, int) else pl.multiple_of(c * L, L)
        xc = xb_ref[bi, pl.ds(t0, L), :]
        if conv_here:
            xc = _causal_conv_silu(prev_ref[bi], xc, cw_ref)
        bdc = bd_ref[bi, pl.ds(t0, L), :]
        beta_all = jax.nn.sigmoid(bdc)
        g_all = neg_a * _softplus(bdc + dtb)
        if valid_len is not None:
            ok = (tok + (j * tb + t0)) < valid_len
            beta_all = jnp.where(ok, beta_all, 0.0)
            g_all = jnp.where(ok, g_all, 0.0)
        g1 = g_all.astype(BF16)
        r1 = g_all - g1.astype(F32)
        g2 = r1.astype(BF16)
        g3 = (r1 - g2.astype(F32)).astype(BF16)
        gc_all = _dot(tril, g1) + _dot(tril, g2) + _dot(tril, g3)
        heads = []
        for h in range(N_HEADS_B):
            q = xc[:, h * hd:(h + 1) * hd]
            k = xc[:, WIDTH_B + h * hd:WIDTH_B + (h + 1) * hd]
            v = xc[:, 2 * WIDTH_B + h * hd:2 * WIDTH_B + (h + 1) * hd]
            q = q * lax.rsqrt(jnp.sum(q * q, axis=-1, keepdims=True) + EPS) * (hd ** -0.5)
            k = k * lax.rsqrt(jnp.sum(k * k, axis=-1, keepdims=True) + EPS)
            beta = jnp.broadcast_to(beta_all[:, h:h + 1], (L, hd))
            gc = jnp.broadcast_to(gc_all[:, N_HEADS_B + h:N_HEADS_B + h + 1], (L, hd))
            eg = jnp.exp(gc)
            g_last = gc[L - 1:L, :]
            kbeta = k * beta
            heads.append(dict(q=q, k=k, kbeta=kbeta, vbeta=v * beta, kbeg=kbeta * eg, qg=q * eg,
                              kdec=k * jnp.exp(g_last - gc), gc=gc, eg_last=jnp.exp(g_last)))
        return heads

    def phase1(jobs):
        heads = [prep_chunk(bi, c) for bi, c in jobs]
        units = [(g, p) for g in range(len(jobs)) for p in range(n_pairs)]
        gram = {}
        for (g, p) in units:
            a, b = heads[g][2 * p], heads[g][2 * p + 1]
            rhs = _lane_blocks(a["k"], b["k"]).astype(BF16)
            lhs = jnp.concatenate([jnp.concatenate([a["kbeta"], b["kbeta"]], axis=-1),
                                   jnp.concatenate([a["q"], b["q"]], axis=-1)], axis=0).astype(BF16)
            gram[g, p] = _dot_nt(lhs, rhs)
        a_mat, intra, t_inv = {}, {}, {}
        for (g, p) in units:
            a, b = heads[g][2 * p], heads[g][2 * p + 1]
            gc_i = jnp.where(low, a["gc"], b["gc"])
            gc_j = jnp.concatenate([a["gc"], b["gc"]], axis=0).T[0:L, :]
            decay = jnp.exp(jnp.where(incl, gc_i - gc_j, -jnp.inf))
            a_mat[g, p] = jnp.where(strict, gram[g, p][0:L] * decay, 0.0)
            intra[g, p] = gram[g, p][L:2 * L] * decay
            t_inv[g, p] = eye - jnp.where(level[0], a_mat[g, p], 0.0)
        for k in range(1, n_levels):
            y = {u: pair_prod(jnp.where(level[k], a_mat[u], 0.0), t_inv[u]) for u in units}
            t_inv = {u: t_inv[u] - pair_prod(t_inv[u], y[u]) for u in units}

        def solve_rhs(x, y):
            z = jnp.zeros((L, hd), BF16)
            return jnp.concatenate([jnp.concatenate([x[0:L], z, x[L:2 * L], z], axis=-1),
                                    jnp.concatenate([z, y[0:L], z, y[L:2 * L]], axis=-1)], axis=0)

        for (g, p) in units:
            a, b = heads[g][2 * p], heads[g][2 * p + 1]
            th, tl = _split2(t_inv[g, p])
            ah, al = _split2(jnp.concatenate([a["vbeta"], a["kbeg"]], axis=0))
            bh, bl = _split2(jnp.concatenate([b["vbeta"], b["kbeg"]], axis=0))
            rh, rl = solve_rhs(ah, bh), solve_rhs(al, bl)
            sol = _dot(th, rh) + _dot(th, rl) + _dot(tl, rh)
            bi, c = jobs[g]
            u_scr[bi, c, p] = sol[:, 0:2 * hd]
            wq_scr[bi, c, p] = jnp.concatenate(
                [sol[:, 2 * hd:4 * hd], jnp.concatenate([a["qg"], b["qg"]], axis=-1)], axis=0).astype(BF16)
            kdec_t = jnp.concatenate([a["kdec"], b["kdec"]], axis=0).T
            ik_scr[bi, c, p] = jnp.concatenate([intra[g, p], kdec_t], axis=0).astype(BF16)
            gl_scr[bi, c, p] = jnp.broadcast_to(
                jnp.concatenate([a["eg_last"], b["eg_last"]], axis=-1), (HIST_ROWS, 2 * hd))

    if nb * n_chunks <= DELTA_GROUP:
        phase1([(bi, c) for bi in range(nb) for c in range(n_chunks)])
    else:
        def group_of(bi, gi, carry):
            phase1([(bi, gi * group + g) for g in range(group)])
            return carry

        for bi in range(nb):
            lax.fori_loop(0, n_chunks // group, functools.partial(group_of, bi), 0)

    def phase2(c, carry):
        t0 = pl.multiple_of(c * L, L)
        units = [(bi, p) for bi in range(nb) for p in range(n_pairs)]
        r1, r2 = {}, {}
        for u in units:
            s = s_scr[u]
            r1[u] = _dot(wq_scr[u[0], c, u[1]], _lane_blocks(s[:, 0:hd], s[:, hd:2 * hd]).astype(BF16))
        for u in units:
            v_new = u_scr[u[0], c, u[1]] - r1[u][0:L]
            r2[u] = _dot(ik_scr[u[0], c, u[1]], _lane_blocks(v_new[:, 0:hd], v_new[:, hd:2 * hd]).astype(BF16))
        for u in units:
            bi, p = u
            o_ref[bi, pl.ds(t0, L), p * 2 * hd:(p + 1) * 2 * hd] = r1[u][L:2 * L] + r2[u][0:L]
            s_scr[u] = s_scr[u] * gl_scr[bi, c, p][0:1, :] + r2[u][L:L + hd]
        return carry

    lax.fori_loop(0, n_chunks, phase2, 0)

    @pl.when(j == pl.num_programs(1) - 1)
    def _():
        if conv_here:
            last = L if valid_len is None else valid_len
            conv_ref[...] = xb_ref[:, last - (CONV_W - 1):last, :]
        for bi in range(nb):
            for h in range(N_HEADS_B):
                s_ref[bi, h] = s_scr[bi, h // 2][:, (h % 2) * hd:(h % 2 + 1) * hd]


def _delta(xb, bd, s0, alog_row, dtb_row, prev=None, conv_w=None):
    b, t, cdim = xb.shape
    conv_here = conv_w is not None
    valid_len = None
    if t % CHUNK:
        assert CONV_W - 1 <= t < CHUNK
        valid_len = t
        xb = jnp.pad(xb, ((0, 0), (0, CHUNK - t), (0, 0)))
        bd = jnp.pad(bd, ((0, 0), (0, CHUNK - t), (0, 0)))
    tp = xb.shape[1]
    tb = _token_tile(tp, DELTA_BLOCK)
    n_chunks = tb // CHUNK
    group = _token_tile(n_chunks, DELTA_GROUP)
    n_pairs = N_HEADS_B // 2
    nb = _token_tile(b, DELTA_SEQS if n_chunks > 1 else DELTA_GROUP // 2)
    tok = lambda wdt: pl.BlockSpec((nb, tb, wdt), lambda i, j: (i, j, 0))
    state = pl.BlockSpec((nb, N_HEADS_B, HEAD_DIM_B, HEAD_DIM_B), lambda i, j: (i, 0, 0, 0))
    conv_state = pl.BlockSpec((nb, CONV_W - 1, cdim), lambda i, j: (i, 0, 0))
    o_shape = jax.ShapeDtypeStruct((b, tp, WIDTH_B), F32)
    s_shape = jax.ShapeDtypeStruct((b, N_HEADS_B, HEAD_DIM_B, HEAD_DIM_B), F32)
    if conv_here:
        assert tp == CHUNK
        prev = jnp.pad(prev, ((0, 0), (HIST_ROWS - (CONV_W - 1), 0), (0, 0)))
        args = (xb, bd, prev, s0, conv_w, alog_row, dtb_row)
        in_specs = [tok(cdim), tok(LANES), pl.BlockSpec((nb, HIST_ROWS, cdim), lambda i, j: (i, 0, 0)), state,
                    _resident(conv_w.shape), _resident((1, LANES)), _resident((1, LANES))]
        out_shape = (o_shape, jax.ShapeDtypeStruct((b, CONV_W - 1, cdim), F32), s_shape)
        out_specs = (tok(WIDTH_B), conv_state, state)
    else:
        args = (xb, bd, s0, alog_row, dtb_row)
        in_specs = [tok(cdim), tok(LANES), state, _resident((1, LANES)), _resident((1, LANES))]
        out_shape = (o_shape, s_shape)
        out_specs = (tok(WIDTH_B), state)
    outs = pl.pallas_call(
        functools.partial(_delta_body, valid_len=valid_len, group=group, conv_here=conv_here),
        out_shape=out_shape,
        grid=(b // nb, tp // tb),
        in_specs=in_specs,
        out_specs=out_specs,
        scratch_shapes=[
            pltpu.VMEM((nb, n_pairs, HEAD_DIM_B, 2 * HEAD_DIM_B), F32),
            pltpu.VMEM((nb, n_chunks, n_pairs, CHUNK, 2 * HEAD_DIM_B), F32),
            pltpu.VMEM((nb, n_chunks, n_pairs, 2 * CHUNK, 2 * HEAD_DIM_B), BF16),
            pltpu.VMEM((nb, n_chunks, n_pairs, CHUNK + HEAD_DIM_B, 2 * CHUNK), BF16),
            pltpu.VMEM((nb, n_chunks, n_pairs, HIST_ROWS, 2 * HEAD_DIM_B), F32),
        ],
        compiler_params=_params(2),
        name="gated_delta",
    )(*args)
    if conv_here:
        o, conv, s = outs
        return o[:, :t], conv, s
    o, s = outs
    return o[:, :t], s


def _memkv_body(m_ref, g_ref, wk_ref, wv_ref, k_ref, v_ref, k16_ref, v16_ref):
    h = _rms(m_ref[...], g_ref[...]).astype(BF16)
    k = _dot(h, wk_ref[...])
    v = _dot(h, wv_ref[...])
    k_ref[...] = k
    v_ref[...] = v
    k16_ref[...] = k.astype(BF16)
    v16_ref[...] = v.astype(BF16)


def _memkv(mem, g, wk, wv, tile=512):
    n, d = mem.shape
    tm = _token_tile(n, tile)
    tok = pl.BlockSpec((tm, d), lambda i: (i, 0))
    return pl.pallas_call(
        _memkv_body,
        out_shape=(jax.ShapeDtypeStruct((n, d), F32),) * 2 + (jax.ShapeDtypeStruct((n, d), BF16),) * 2,
        grid=(n // tm,),
        in_specs=[tok, _resident((1, d)), _resident(wk.shape), _resident(wv.shape)],
        out_specs=(tok,) * 4,
        compiler_params=_params(1),
        name="mem_kv",
    )(mem, g, wk, wv)


POST_ROWS = 128


def _post_body(x_ref, ya_ref, ob_ref, mk_ref, mv_ref, mixg_ref, on_ref, xg_ref,
               wz_ref, wgate_ref, wa_ref, wb_ref, wm_ref, wq_ref, wo_ref, o_ref):
    nb, tq, d = x_ref.shape
    x = x_ref[...].reshape(nb * tq, d)
    hd = HEAD_DIM_B
    h = _rms(x, mixg_ref[...]).astype(BF16)
    z = _dot(h, wz_ref[...])
    gate = jax.nn.sigmoid(_dot(h, wgate_ref[...]))
    ob = ob_ref[...].reshape(nb * tq, WIDTH_B)
    parts = []
    for i in range(N_HEADS_B):
        sl = slice(i * hd, (i + 1) * hd)
        parts.append(_rms(ob[:, sl], on_ref[...]) * _silu(z[:, sl]))
    obn = jnp.concatenate(parts, axis=-1).astype(BF16)
    ya = ya_ref[...].reshape(nb * tq, WIDTH_A)
    merged = gate[:, :d] * _dot(ya, wa_ref[...]) + gate[:, d:] * _dot(obn, wb_ref[...])
    x = x + _dot(merged.astype(BF16), wm_ref[...])

    hx = d // N_HEADS_X
    q = (_dot(_rms(x, xg_ref[...]).astype(BF16), wq_ref[...]) * (hx ** -0.5 * LOG2E)).astype(BF16)
    units = [(bi, slice(i * hx, (i + 1) * hx)) for bi in range(nb) for i in range(N_HEADS_X)]
    scores = [_dot_nt(q[bi * tq:(bi + 1) * tq, sl], mk_ref[bi, :, sl]) for bi, sl in units]
    probs, inv = [], []
    for s in scores:
        p = jnp.exp2(s - jnp.max(s, axis=-1, keepdims=True))
        inv.append(1.0 / jnp.sum(p, axis=-1, keepdims=True))
        probs.append(p.astype(BF16))
    outs = [_dot(p, mv_ref[bi, :, sl]) * r for p, r, (bi, sl) in zip(probs, inv, units)]
    rows = [jnp.concatenate(outs[bi * N_HEADS_X:(bi + 1) * N_HEADS_X], axis=-1) for bi in range(nb)]
    o = jnp.concatenate(rows, axis=0).astype(BF16)
    o_ref[...] = (x + _dot(o, wo_ref[...])).reshape(nb, tq, d)


def _post_mixer(x, ya, ob, mk, mv, lp, tile=1024):
    b, t, d = x.shape
    n_mem = mk.shape[1]
    tq = _token_tile(t, tile)
    nb = _token_tile(b, max(1, POST_ROWS // tq))
    tok = lambda wdt: pl.BlockSpec((nb, tq, wdt), lambda i, j: (i, j, 0))
    mem = pl.BlockSpec((nb, n_mem, d), lambda i, j: (i, 0, 0))
    weights = [lp["mix_norm"], lp["b_out_norm"], lp["xattn_norm"], lp["w_z"], lp["w_gate"],
               lp["w_branch_a"], lp["w_branch_b"], lp["w_mix_out"], lp["xattn_wq"], lp["xattn_wo"]]
    return pl.pallas_call(
        _post_body,
        out_shape=jax.ShapeDtypeStruct((b, t, d), F32),
        grid=(b // nb, t // tq),
        in_specs=[tok(d), tok(WIDTH_A), tok(WIDTH_B), mem, mem] + [_resident(w.shape) for w in weights],
        out_specs=tok(d),
        compiler_params=_params(2),
        name="post_mixer",
    )(x, ya, ob, mk, mv, *weights)


def _rel_bias(table, q_len, k_first, n_keys):
    d_min = -(k_first + n_keys - 1)
    d_max = q_len - 1 - k_first
    core = table[:, max(d_min, -REL_CLIP) + REL_CLIP:min(d_max, REL_CLIP) + REL_CLIP + 1]
    ext = jnp.pad(core, ((0, 0), (max(0, -REL_CLIP - d_min), max(0, d_max - REL_CLIP))), mode="edge")
    rev = ext[:, ::-1]
    m = rev.shape[1]
    assert m == n_keys + q_len - 1 and q_len >= 2
    flat = jnp.tile(jnp.roll(rev, -(q_len - 1), axis=1), (1, q_len))[:, :q_len * (m - 1)]
    return flat.reshape(-1, q_len, m - 1)[:, :, :n_keys].astype(F32) * LOG2E


def _band_bias(table):
    t = jnp.arange(Q_TILE)
    kk = jnp.arange(K_TILE)
    bias = _rel_bias(table, Q_TILE, -A_REACH, K_TILE)
    q_chunk = t[:, None] // CHUNK
    k_chunk = kk[None, :] // CHUNK
    in_band = (k_chunk >= q_chunk) & (k_chunk <= q_chunk + A_REACH // CHUNK)
    bias = jnp.where(in_band[None], bias, -jnp.inf)
    return bias.reshape(N_HEADS_A // 2, 2 * Q_TILE, K_TILE)


def _trunk(x, lp, mem_k, mem_v, a_k_cache, a_v_cache, conv_prev, s0):
    b, t, d = x.shape
    n = b * t
    x = _ffn(x.reshape(n, d), lp["ffn1_norm"], lp["ffn1_wg"], lp["ffn1_wu"], lp["ffn1_wd"])
    if a_k_cache is None:
        qa, ka, va, xc, bd, tail = _proj(x, lp["mix_norm"], lp["w_cat"], conv_w=lp["b_conv_w"], seq_len=t)
        qa, ka, va = (a.reshape(b, t, WIDTH_A) for a in (qa, ka, va))
        ya = _attn_prompt(qa, ka, va, _band_bias(lp["a_rel_bias"]))
        keep = min(A_REACH, t)
        new_k, new_v = ka[:, t - keep:], va[:, t - keep:]
        ob, new_s = _delta(xc.reshape(b, t, B_CONV_DIM), bd.reshape(b, t, LANES), s0,
                           lp["alog_row"], lp["dtb_row"])
        new_conv = tail[:, HIST_ROWS - (CONV_W - 1):]
    else:
        qa, ka, va, xb, bd = _proj(x, lp["mix_norm"], lp["w_cat"])
        qa, ka, va = (a.reshape(b, t, WIDTH_A) for a in (qa, ka, va))
        p_len = a_k_cache.shape[1]
        ya = _attn_sample(
            qa, ka, va, a_k_cache.reshape(b, p_len, WIDTH_A), a_v_cache.reshape(b, p_len, WIDTH_A),
            _rel_bias(lp["a_rel_bias"], t, -p_len, p_len).reshape(N_HEADS_A // 2, 2 * t, p_len),
            _rel_bias(lp["a_rel_bias"], t, 0, t).reshape(N_HEADS_A // 2, 2 * t, t))
        new_k, new_v = ka, va
        ob, new_conv, new_s = _delta(xb.reshape(b, t, B_CONV_DIM), bd.reshape(b, t, LANES), s0,
                                     lp["alog_row"], lp["dtb_row"], prev=conv_prev, conv_w=lp["b_conv_w"])
    x = _post_mixer(x.reshape(b, t, d), ya, ob, mem_k, mem_v, lp)
    y = _ffn(x.reshape(n, d), lp["ffn2_norm"], lp["ffn2_wg"], lp["ffn2_wu"], lp["ffn2_wd"],
             final_g=lp["final_norm"])
    heads = (b, -1, N_HEADS_A, HEAD_DIM_A)
    return y.reshape(b, t, d), new_k.reshape(heads), new_v.reshape(heads), new_conv, new_s


def kernel(x_prompt, x_sample, cache_a_k, cache_a_v, state_b_conv, state_b_s, cache_mem_k, cache_mem_v, mem_prompt, ffn1_norm, ffn1_w_gate, ffn1_w_up, ffn1_w_down, mix_norm, w_in, a_rel_bias, b_conv_w, b_a_log, b_dt_bias, b_out_norm, w_branch_a, w_branch_b, w_mix_out, xattn_norm, mem_norm, xattn_wq, xattn_wk, xattn_wv, xattn_wo, ffn2_norm, ffn2_w_gate, ffn2_w_up, ffn2_w_down, final_norm):
    depth = ffn1_norm.shape[0]
    assert depth == 1
    l = 0
    d = x_prompt.shape[-1]
    bp = x_prompt.shape[0]
    n_mem = mem_prompt.shape[1]

    w = w_in[l]
    off_b = 3 * WIDTH_A
    off_z = off_b + B_CONV_DIM
    off_beta = off_z + WIDTH_B
    off_gate = off_beta + 2 * N_HEADS_B
    small = jnp.pad(w[:, off_beta:off_gate], ((0, 0), (0, LANES - 2 * N_HEADS_B)))
    w_cat = jnp.concatenate([w[:, :off_z], small], axis=1).astype(BF16)

    def lane_row(vec):
        return jnp.pad(vec.astype(F32), (N_HEADS_B, LANES - 2 * N_HEADS_B)).reshape(1, LANES)

    def row(vec):
        return vec.astype(F32).reshape(1, -1)

    lp = {
        "ffn1_norm": row(ffn1_norm[l]), "ffn1_wg": ffn1_w_gate[l].astype(BF16),
        "ffn1_wu": ffn1_w_up[l].astype(BF16), "ffn1_wd": ffn1_w_down[l].astype(BF16),
        "mix_norm": row(mix_norm[l]), "w_cat": w_cat, "w_z": w[:, off_z:off_beta].astype(BF16),
        "w_gate": w[:, off_gate:].astype(BF16), "a_rel_bias": a_rel_bias[l],
        "b_conv_w": b_conv_w[l], "alog_row": lane_row(b_a_log[l]), "dtb_row": lane_row(b_dt_bias[l]),
        "b_out_norm": row(b_out_norm[l]), "w_branch_a": w_branch_a[l].astype(BF16),
        "w_branch_b": w_branch_b[l].astype(BF16), "w_mix_out": w_mix_out[l].astype(BF16),
        "xattn_norm": row(xattn_norm[l]), "xattn_wq": xattn_wq[l].astype(BF16),
        "xattn_wo": xattn_wo[l].astype(BF16), "ffn2_norm": row(ffn2_norm[l]),
        "ffn2_wg": ffn2_w_gate[l].astype(BF16), "ffn2_wu": ffn2_w_up[l].astype(BF16),
        "ffn2_wd": ffn2_w_down[l].astype(BF16), "final_norm": row(final_norm),
    }

    mk_p, mv_p, mk16, mv16 = _memkv(mem_prompt.reshape(bp * n_mem, d), row(mem_norm[l]),
                                    xattn_wk[l].astype(BF16), xattn_wv[l].astype(BF16))
    s_zero = jnp.zeros((bp, N_HEADS_B, HEAD_DIM_B, HEAD_DIM_B), F32)
    yp, p_ak, p_av, p_cv, p_sb = _trunk(x_prompt, lp, mk16.reshape(bp, n_mem, d), mv16.reshape(bp, n_mem, d),
                                        None, None, None, s_zero)

    bs = x_sample.shape[0]
    ys, s_ak, s_av, s_cv, s_sb = _trunk(
        x_sample, lp, cache_mem_k[l].reshape(bs, n_mem, d).astype(BF16),
        cache_mem_v[l].reshape(bs, n_mem, d).astype(BF16),
        cache_a_k[l], cache_a_v[l], state_b_conv[l], state_b_s[l])

    mem_heads = (1, bp, n_mem, N_HEADS_X, d // N_HEADS_X)
    return (yp, ys, p_ak[None], p_av[None], p_cv[None], p_sb[None],
            mk_p.reshape(mem_heads), mv_p.reshape(mem_heads),
            s_ak[None], s_av[None], s_cv[None], s_sb[None])
```
